```python
import jax
import jax.numpy as jnp
from jax import lax

D_MODEL = 2048
BATCH = 4
SEQ = 4096
DEPTH = 4

CTX_LEN = 256
GRID_W = 64
HEAD_DIM = 128
N_HEADS = 8
N_KV_HEADS = 2
Q_PER_KV = N_HEADS // N_KV_HEADS
ATTN_WIDTH = N_HEADS * HEAD_DIM
KV_WIDTH = N_KV_HEADS * HEAD_DIM
CONV_WIDTH = D_MODEL - ATTN_WIDTH
CONV_K = 3
IN_WIDTH = ATTN_WIDTH + 2 * KV_WIDTH + 3 * CONV_WIDTH
IN_SPLITS = (
    ATTN_WIDTH,
    ATTN_WIDTH + KV_WIDTH,
    ATTN_WIDTH + 2 * KV_WIDTH,
    ATTN_WIDTH + 2 * KV_WIDTH + CONV_WIDTH,
    ATTN_WIDTH + 2 * KV_WIDTH + 2 * CONV_WIDTH,
)
MLP_HIDDEN = 4 * D_MODEL
N_MOD = 6
ROPE_THETA = 10000.0
AXIS_DIM = HEAD_DIM // 2
BLOCK_Q = 128
EPS = 1e-6

kernel_name = 'hybrid_dit_parallel_shortconv_gqa'


def rms_norm(x, g):
    x32 = x.astype(jnp.float32)
    y = x32 * lax.rsqrt(jnp.mean(x32 * x32, axis=-1, keepdims=True) + EPS)
    return (y * g.astype(jnp.float32)).astype(x.dtype)


def modulate(h, shift, scale):
    return h * (1 + scale) + shift


def adaln(cond, w, b):
    return [m[..., None, :] for m in jnp.split(jax.nn.silu(cond) @ w + b, N_MOD, axis=-1)]


def axial_rope_tables(rows):
    row = jnp.broadcast_to(jnp.arange(rows, dtype=jnp.float32)[:, None], (rows, GRID_W)).reshape(-1)
    col = jnp.broadcast_to(jnp.arange(GRID_W, dtype=jnp.float32)[None, :], (rows, GRID_W)).reshape(-1)
    inv_freq = ROPE_THETA ** (-jnp.arange(0, AXIS_DIM, 2, dtype=jnp.float32) / AXIS_DIM)
    ang_r = row[:, None] * inv_freq[None, :]
    ang_c = col[:, None] * inv_freq[None, :]
    ang_r = jnp.concatenate([ang_r, ang_r], axis=-1)
    ang_c = jnp.concatenate([ang_c, ang_c], axis=-1)
    return jnp.cos(ang_r), jnp.sin(ang_r), jnp.cos(ang_c), jnp.sin(ang_c)


def _rotate_axis(x, cos, sin):
    x1, x2 = jnp.split(x, 2, axis=-1)
    return x * cos + jnp.concatenate([-x2, x1], axis=-1) * sin


def apply_axial_rope(x, tables):
    cos_r, sin_r, cos_c, sin_c = [t[None, :, None, :] for t in tables]
    xr, xc = jnp.split(x.astype(jnp.float32), 2, axis=-1)
    out = jnp.concatenate([_rotate_axis(xr, cos_r, sin_r), _rotate_axis(xc, cos_c, sin_c)], axis=-1)
    return out.astype(x.dtype)


def gqa_attend(q, k, v):
    s = jnp.einsum('bqhrd,bthd->bhrqt', q, k, preferred_element_type=jnp.float32) * (HEAD_DIM ** -0.5)
    p = jax.nn.softmax(s, axis=-1).astype(v.dtype)
    return jnp.einsum('bhrqt,bthd->bqhrd', p, v)


def latent_attention(q, k, v):
    b, s = q.shape[0], q.shape[1]
    n_blocks = s // BLOCK_Q
    qb = q.reshape(b, n_blocks, BLOCK_Q, N_KV_HEADS, Q_PER_KV, HEAD_DIM).swapaxes(0, 1)
    out = lax.map(lambda blk: gqa_attend(blk, k, v), qb)
    return out.swapaxes(0, 1).reshape(b, s, ATTN_WIDTH)


def context_attention(q, k, v):
    b, t = q.shape[0], q.shape[1]
    qg = q.reshape(b, t, N_KV_HEADS, Q_PER_KV, HEAD_DIM)
    return gqa_attend(qg, k, v).reshape(b, t, ATTN_WIDTH)


def mixer_inputs(h, w_in, q_g, k_g):
    b, t = h.shape[0], h.shape[1]
    q, k, v, gate_b, gate_c, u = jnp.split(h @ w_in, list(IN_SPLITS), axis=-1)
    q = rms_norm(q.reshape(b, t, N_HEADS, HEAD_DIM), q_g)
    k = rms_norm(k.reshape(b, t, N_KV_HEADS, HEAD_DIM), k_g)
    v = v.reshape(b, t, N_KV_HEADS, HEAD_DIM)
    return q, k, v, gate_b, gate_c, u


def short_conv_mixer(gate_b, gate_c, u, w, bias):
    z = jnp.pad(gate_c * u, ((0, 0), (1, 1), (0, 0)))
    conv = z[:, :-2] * w[0] + z[:, 1:-1] * w[1] + z[:, 2:] * w[2] + bias
    return gate_b * conv


def merge_heads(attn, conv, attn_g, conv_g, w_out):
    return jnp.concatenate([rms_norm(attn, attn_g), rms_norm(conv, conv_g)], axis=-1) @ w_out


def sq_relu_mlp(h, w1, w2):
    return jnp.square(jax.nn.relu(h @ w1)) @ w2


def setup_inputs(seed: int = 0) -> dict:
    key = jax.random.key(seed)
    ks = jax.random.split(key, 18)
    f32 = jnp.float32

    def nrm(k, shape, scale):
        return jax.random.normal(k, shape, f32) * scale

    def gain(k, shape):
        return 1.0 + nrm(k, shape, 0.05)

    return {
        'x': nrm(ks[0], (BATCH, SEQ, D_MODEL), 1.0),
        'c': nrm(ks[1], (BATCH, D_MODEL), 1.0),
        'ctx': nrm(ks[2], (BATCH, CTX_LEN, D_MODEL), 1.0),
        'c_ctx': nrm(ks[3], (D_MODEL,), 1.0),
        'w_ada': nrm(ks[4], (DEPTH, D_MODEL, N_MOD * D_MODEL), 0.5 * D_MODEL ** -0.5),
        'b_ada': nrm(ks[5], (DEPTH, N_MOD * D_MODEL), 0.02),
        'norm1_g': gain(ks[6], (DEPTH, D_MODEL)),
        'w_in': nrm(ks[7], (DEPTH, D_MODEL, IN_WIDTH), D_MODEL ** -0.5),
        'q_norm_g': gain(ks[8], (DEPTH, HEAD_DIM)),
        'k_norm_g': gain(ks[9], (DEPTH, HEAD_DIM)),
        'conv_w': nrm(ks[10], (DEPTH, CONV_K, CONV_WIDTH), CONV_K ** -0.5),
        'conv_b': nrm(ks[11], (DEPTH, CONV_WIDTH), 0.02),
        'attn_out_g': gain(ks[12], (DEPTH, ATTN_WIDTH)),
        'conv_out_g': gain(ks[13], (DEPTH, CONV_WIDTH)),
        'w_out': nrm(ks[14], (DEPTH, D_MODEL, D_MODEL), D_MODEL ** -0.5),
        'norm2_g': gain(ks[15], (DEPTH, D_MODEL)),
        'w_mlp_in': nrm(ks[16], (DEPTH, D_MODEL, MLP_HIDDEN), D_MODEL ** -0.5),
        'w_mlp_out': nrm(ks[17], (DEPTH, MLP_HIDDEN, D_MODEL), MLP_HIDDEN ** -0.5),
    }


def reference(x, c, ctx, c_ctx, w_ada, b_ada, norm1_g, w_in, q_norm_g, k_norm_g,
              conv_w, conv_b, attn_out_g, conv_out_g, w_out, norm2_g, w_mlp_in, w_mlp_out):
    rows = x.shape[1] // GRID_W
    tables = axial_rope_tables(rows)
    for l in range(DEPTH):
        sh1, sc1, g1, sh2, sc2, g2 = adaln(c, w_ada[l], b_ada[l])
        csh1, csc1, cg1, csh2, csc2, cg2 = adaln(c_ctx, w_ada[l], b_ada[l])

        hx = modulate(rms_norm(x, norm1_g[l]), sh1, sc1)
        hc = modulate(rms_norm(ctx, norm1_g[l]), csh1, csc1)
        qx, kx, vx, bx, cx, ux = mixer_inputs(hx, w_in[l], q_norm_g[l], k_norm_g[l])
        qc, kc, vc, bc, cc, uc = mixer_inputs(hc, w_in[l], q_norm_g[l], k_norm_g[l])

        qx = apply_axial_rope(qx, tables)
        kx = apply_axial_rope(kx, tables)
        k_all = jnp.concatenate([kx, kc], axis=1)
        v_all = jnp.concatenate([vx, vc], axis=1)
        attn_x = latent_attention(qx, k_all, v_all)
        conv_x = short_conv_mixer(bx, cx, ux, conv_w[l], conv_b[l])
        x_new = x + g1 * merge_heads(attn_x, conv_x, attn_out_g[l], conv_out_g[l], w_out[l])
        x_new = x_new + g2 * sq_relu_mlp(modulate(rms_norm(x_new, norm2_g[l]), sh2, sc2),
                                         w_mlp_in[l], w_mlp_out[l])

        if l < DEPTH - 1:
            attn_c = context_attention(qc, kc, vc)
            conv_c = short_conv_mixer(bc, cc, uc, conv_w[l], conv_b[l])
            ctx = ctx + cg1 * merge_heads(attn_c, conv_c, attn_out_g[l], conv_out_g[l], w_out[l])
            ctx = ctx + cg2 * sq_relu_mlp(modulate(rms_norm(ctx, norm2_g[l]), csh2, csc2),
                                          w_mlp_in[l], w_mlp_out[l])
        x = x_new
    return x
```

```python
import functools

import jax
import jax.numpy as jnp
from jax import lax
from jax.experimental import pallas as pl
from jax.experimental.pallas import tpu as pltpu

D_MODEL = 2048
BATCH = 4
SEQ = 4096
DEPTH = 4
CTX_LEN = 256
GRID_W = 64
HEAD_DIM = 128
N_HEADS = 8
N_KV_HEADS = 2
Q_PER_KV = N_HEADS // N_KV_HEADS
ATTN_WIDTH = N_HEADS * HEAD_DIM
KV_WIDTH = N_KV_HEADS * HEAD_DIM
CONV_WIDTH = D_MODEL - ATTN_WIDTH
IN_WIDTH = ATTN_WIDTH + 2 * KV_WIDTH + 3 * CONV_WIDTH
MLP_HIDDEN = 4 * D_MODEL
N_MOD = 6
ROPE_THETA = 10000.0
AXIS_DIM = HEAD_DIM // 2
EPS = 1e-6

N_LATENT = BATCH * SEQ
N_CTX = BATCH * CTX_LEN
N_TOKENS = N_LATENT + N_CTX
MOD_ROWS = 8

F32 = jnp.float32
BF16 = jnp.bfloat16

ADA_TN = 1024
INPROJ_TM = 1024
INPROJ_TN = 512
ATTN_TQ = 256
ATTN_TK = 512
MERGE_TM = 256
MLP_TM = 512
MLP_TH = 512
HALO = 8
VMEM_LIMIT = 56 * 1024 * 1024


def _params(n_axes, vmem=VMEM_LIMIT):
    return pltpu.CompilerParams(dimension_semantics=("arbitrary",) * n_axes, vmem_limit_bytes=vmem)


def _mod_row(tile, tm):
    return jnp.minimum(tile // (SEQ // tm), BATCH)


def _rms(x, g):
    return x * lax.rsqrt(jnp.mean(x * x, axis=-1, keepdims=True) + EPS) * g


def _ada_kernel(cond_ref, w_ref, b_ref, o_ref):
    cnd = cond_ref[...]
    s = (cnd * jax.nn.sigmoid(cnd)).astype(BF16)
    w = w_ref[...].astype(BF16)
    o_ref[...] = jnp.dot(s, w, preferred_element_type=F32) + b_ref[...]


def _adaln_all(cond, w_ada, b_ada):
    n = N_MOD * D_MODEL
    return pl.pallas_call(
        _ada_kernel,
        grid=(DEPTH, n // ADA_TN),
        in_specs=[
            pl.BlockSpec((MOD_ROWS, D_MODEL), lambda l, j: (0, 0)),
            pl.BlockSpec((None, D_MODEL, ADA_TN), lambda l, j: (l, 0, j)),
            pl.BlockSpec((None, 1, ADA_TN), lambda l, j: (l, 0, j)),
        ],
        out_specs=pl.BlockSpec((None, MOD_ROWS, ADA_TN), lambda l, j: (l, 0, j)),
        out_shape=jax.ShapeDtypeStruct((DEPTH, MOD_ROWS, n), F32),
        compiler_params=_params(2),
        name="adaln",
    )(cond, w_ada, b_ada.reshape(DEPTH, 1, n))


_J_KV = ATTN_WIDTH // INPROJ_TN
_J_Z0 = _J_KV + 1
_ZC = INPROJ_TN // 2
_J_GB0 = _J_Z0 + CONV_WIDTH // _ZC
_J_END = _J_GB0 + CONV_WIDTH // INPROJ_TN


def _rope(x, cos, sin_a, sin_b):
    return (x * cos + pltpu.roll(x, HEAD_DIM - AXIS_DIM // 2, axis=1) * sin_a
            + pltpu.roll(x, AXIS_DIM // 2, axis=1) * sin_b)


def _inproj_kernel(x_ref, sh_ref, sc_ref, n1g_ref, w_ref, qg_ref, kg_ref, cos_ref, sa_ref, sb_ref,
                   q_ref, kv_ref, z_ref, gb_ref, h_scr):
    j = pl.program_id(1)

    @pl.when(j == 0)
    def _():
        h = _rms(x_ref[...], n1g_ref[...]) * (1.0 + sc_ref[...]) + sh_ref[...]
        h_scr[...] = h.astype(BF16)

    r = jnp.dot(h_scr[...], w_ref[...], preferred_element_type=F32)

    def heads(r_cols, gain, scale):
        out = []
        for hd in range(r_cols.shape[1] // HEAD_DIM):
            y = _rms(r_cols[:, hd * HEAD_DIM:(hd + 1) * HEAD_DIM], gain)
            y = _rope(y, cos_ref[...], sa_ref[...], sb_ref[...])
            out.append((y * scale).astype(BF16))
        return out

    @pl.when(j < _J_KV)
    def _():
        for hd, y in enumerate(heads(r, qg_ref[...], HEAD_DIM ** -0.5)):
            q_ref[:, hd * HEAD_DIM:(hd + 1) * HEAD_DIM] = y

    @pl.when(j == _J_KV)
    def _():
        for hd, y in enumerate(heads(r[:, :KV_WIDTH], kg_ref[...], 1.0)):
            kv_ref[:, hd * HEAD_DIM:(hd + 1) * HEAD_DIM] = y
        kv_ref[:, KV_WIDTH:] = r[:, KV_WIDTH:].astype(BF16)

    @pl.when(jnp.logical_and(j >= _J_Z0, j < _J_GB0))
    def _():
        z_ref[...] = r[:, :_ZC] * r[:, _ZC:]

    @pl.when(j >= _J_GB0)
    def _():
        gb_ref[...] = r


def _in_proj(xs, mods, layer, n1g, w_in_r, qg, kg, cos_t, sa_t, sb_t):
    tm, tn = INPROJ_TM, INPROJ_TN
    n_tiles = N_TOKENS // tm
    lat_tiles = SEQ // tm

    def mod_spec(chunk):
        return pl.BlockSpec((None, None, 1, D_MODEL),
                            lambda i, j: (layer, _mod_row(i, tm), 0, chunk))

    def tab_spec():
        return pl.BlockSpec((tm, HEAD_DIM),
                            lambda i, j: (jnp.where(i < N_LATENT // tm, i % lat_tiles, lat_tiles), 0))

    return pl.pallas_call(
        _inproj_kernel,
        grid=(n_tiles, _J_END),
        in_specs=[
            pl.BlockSpec((tm, D_MODEL), lambda i, j: (i, 0)),
            mod_spec(0), mod_spec(1),
            pl.BlockSpec((1, D_MODEL), lambda i, j: (0, 0)),
            pl.BlockSpec((D_MODEL, tn), lambda i, j: (0, j)),
            pl.BlockSpec((1, HEAD_DIM), lambda i, j: (0, 0)),
            pl.BlockSpec((1, HEAD_DIM), lambda i, j: (0, 0)),
            tab_spec(), tab_spec(), tab_spec(),
        ],
        out_specs=[
            pl.BlockSpec((tm, tn), lambda i, j: (i, jnp.minimum(j, _J_KV - 1))),
            pl.BlockSpec((tm, 2 * KV_WIDTH), lambda i, j: (i, 0)),
            pl.BlockSpec((tm, _ZC), lambda i, j: (i, jnp.clip(j - _J_Z0, 0, CONV_WIDTH // _ZC - 1))),
            pl.BlockSpec((tm, tn), lambda i, j: (i, jnp.clip(j - _J_GB0, 0, CONV_WIDTH // tn - 1))),
        ],
        out_shape=[
            jax.ShapeDtypeStruct((N_TOKENS, ATTN_WIDTH), BF16),
            jax.ShapeDtypeStruct((N_TOKENS, 2 * KV_WIDTH), BF16),
            jax.ShapeDtypeStruct((N_TOKENS, CONV_WIDTH), F32),
            jax.ShapeDtypeStruct((N_TOKENS, CONV_WIDTH), F32),
        ],
        scratch_shapes=[pltpu.VMEM((tm, D_MODEL), BF16)],
        compiler_params=_params(2),
        name="in_proj",
    )(xs, mods, mods, n1g, w_in_r, qg, kg, cos_t, sa_t, sb_t)


_NQ = SEQ // ATTN_TQ


def _attend(q, chunks):
    m = l = acc = None
    for k_ref, v_ref, start, size in chunks:
        k = k_ref[start:start + size, :]
        v = v_ref[start:start + size, :]
        s = lax.dot_general(q, k, (((1,), (1,)), ((), ())), preferred_element_type=F32)
        m_c = jnp.max(s, axis=-1, keepdims=True)
        if m is None:
            m = m_c
            p = jnp.exp(s - m)
            l = jnp.sum(p, axis=-1, keepdims=True)
            acc = jnp.dot(p.astype(BF16), v, preferred_element_type=F32)
        else:
            m_new = jnp.maximum(m, m_c)
            alpha = jnp.exp(m - m_new)
            p = jnp.exp(s - m_new)
            l = alpha * l + jnp.sum(p, axis=-1, keepdims=True)
            acc = alpha * acc + jnp.dot(p.astype(BF16), v, preferred_element_type=F32)
            m = m_new
    return acc / l


def _attn_kernel(q_ref, kl_ref, vl_ref, kc_ref, vc_ref, o_ref):
    qi = pl.program_id(2)
    ctx_chunk = [(kc_ref, vc_ref, 0, CTX_LEN)]
    lat_chunks = [(kl_ref, vl_ref, c * ATTN_TK, ATTN_TK) for c in range(SEQ // ATTN_TK)]

    def run(chunks):
        for r in range(Q_PER_KV):
            cols = slice(r * HEAD_DIM, (r + 1) * HEAD_DIM)
            o_ref[:, cols] = _attend(q_ref[:, cols], chunks).astype(BF16)

    @pl.when(qi < _NQ)
    def _():
        run(lat_chunks + ctx_chunk)

    @pl.when(qi == _NQ)
    def _():
        run(ctx_chunk)


def _attention(q, kv):
    tq = ATTN_TQ
    ctx_blk0 = N_LATENT // CTX_LEN

    def q_map(b, g, qi):
        return (jnp.where(qi < _NQ, b * _NQ + qi, N_LATENT // tq + b), g)

    return pl.pallas_call(
        _attn_kernel,
        grid=(BATCH, N_KV_HEADS, _NQ + 1),
        in_specs=[
            pl.BlockSpec((tq, Q_PER_KV * HEAD_DIM), q_map),
            pl.BlockSpec((SEQ, HEAD_DIM), lambda b, g, qi: (b, g)),
            pl.BlockSpec((SEQ, HEAD_DIM), lambda b, g, qi: (b, N_KV_HEADS + g)),
            pl.BlockSpec((CTX_LEN, HEAD_DIM), lambda b, g, qi: (ctx_blk0 + b, g)),
            pl.BlockSpec((CTX_LEN, HEAD_DIM), lambda b, g, qi: (ctx_blk0 + b, N_KV_HEADS + g)),
        ],
        out_specs=pl.BlockSpec((tq, Q_PER_KV * HEAD_DIM), q_map),
        out_shape=jax.ShapeDtypeStruct((N_TOKENS, ATTN_WIDTH), BF16),
        compiler_params=_params(3),
        name="attention",
    )(q, kv, kv, kv, kv)


def _merge_kernel(x_ref, attn_ref, z_ref, zp_ref, zn_ref, gb_ref, cw_ref, cb_ref, ag_ref, cg_ref, w_ref,
                  g1_ref, sh2_ref, sc2_ref, n2g_ref, xo_ref, h2_ref, cat_scr):
    tm = x_ref.shape[0]
    i = pl.program_id(0)

    row = lax.broadcasted_iota(jnp.int32, (tm, 1), 0)
    grow = i * tm + row
    seg = jnp.where(grow >= N_LATENT, CTX_LEN, SEQ)
    pos = jnp.bitwise_and(grow, seg - 1)

    z = z_ref[...]
    z_prev = jnp.where(row == 0, zp_ref[HALO - 1:HALO, :], pltpu.roll(z, 1, axis=0))
    z_prev = jnp.where(pos != 0, z_prev, 0.0)
    z_next = jnp.where(row == tm - 1, zn_ref[0:1, :], pltpu.roll(z, tm - 1, axis=0))
    z_next = jnp.where(pos != seg - 1, z_next, 0.0)
    conv = z_prev * cw_ref[0:1, :] + z * cw_ref[1:2, :] + z_next * cw_ref[2:3, :] + cb_ref[...]
    conv = gb_ref[...] * conv

    cat_scr[:, :ATTN_WIDTH] = _rms(attn_ref[...].astype(F32), ag_ref[...]).astype(BF16)
    cat_scr[:, ATTN_WIDTH:] = _rms(conv, cg_ref[...]).astype(BF16)
    y = jnp.dot(cat_scr[...], w_ref[...], preferred_element_type=F32)
    x_new = x_ref[...] + g1_ref[...] * y
    xo_ref[...] = x_new
    h2 = _rms(x_new, n2g_ref[...]) * (1.0 + sc2_ref[...]) + sh2_ref[...]
    h2_ref[...] = h2.astype(BF16)


def _merge(xs, attn, z, gb, mods, layer, cw, cb, ag, cg, w_out, n2g, n_rows):
    tm = MERGE_TM
    hb = tm // HALO
    last_hb = N_TOKENS // HALO - 1

    def mod_spec(chunk):
        return pl.BlockSpec((None, None, 1, D_MODEL), lambda i: (layer, _mod_row(i, tm), 0, chunk))

    def vec_spec(rows, width):
        return pl.BlockSpec((rows, width), lambda i: (0, 0))

    return pl.pallas_call(
        _merge_kernel,
        grid=(n_rows // tm,),
        in_specs=[
            pl.BlockSpec((tm, D_MODEL), lambda i: (i, 0)),
            pl.BlockSpec((tm, ATTN_WIDTH), lambda i: (i, 0)),
            pl.BlockSpec((tm, CONV_WIDTH), lambda i: (i, 0)),
            pl.BlockSpec((HALO, CONV_WIDTH), lambda i: (jnp.maximum(i * hb - 1, 0), 0)),
            pl.BlockSpec((HALO, CONV_WIDTH), lambda i: (jnp.minimum((i + 1) * hb, last_hb), 0)),
            pl.BlockSpec((tm, CONV_WIDTH), lambda i: (i, 0)),
            vec_spec(3, CONV_WIDTH), vec_spec(1, CONV_WIDTH),
            vec_spec(1, ATTN_WIDTH), vec_spec(1, CONV_WIDTH),
            pl.BlockSpec((D_MODEL, D_MODEL), lambda i: (0, 0)),
            mod_spec(2), mod_spec(3), mod_spec(4),
            vec_spec(1, D_MODEL),
        ],
        out_specs=[
            pl.BlockSpec((tm, D_MODEL), lambda i: (i, 0)),
            pl.BlockSpec((tm, D_MODEL), lambda i: (i, 0)),
        ],
        out_shape=[
            jax.ShapeDtypeStruct((n_rows, D_MODEL), F32),
            jax.ShapeDtypeStruct((n_rows, D_MODEL), BF16),
        ],
        scratch_shapes=[pltpu.VMEM((tm, D_MODEL), BF16)],
        compiler_params=_params(1),
        name="merge_out_proj",
    )(xs, attn, z, z, z, gb, cw, cb, ag, cg, w_out, mods, mods, mods, n2g)


def _mlp_kernel(h_ref, w1_ref, w2_ref, x_ref, g2_ref, o_ref, acc_scr):
    k = pl.program_id(1)
    a = jnp.maximum(jnp.dot(h_ref[...], w1_ref[...], preferred_element_type=F32), 0.0)
    part = jnp.dot((a * a).astype(BF16), w2_ref[...], preferred_element_type=F32)

    @pl.when(k == 0)
    def _():
        acc_scr[...] = part

    @pl.when(k > 0)
    def _():
        acc_scr[...] += part

    @pl.when(k == pl.num_programs(1) - 1)
    def _():
        o_ref[...] = x_ref[...] + g2_ref[...] * acc_scr[...]


def _mlp(h2, x_new, mods, layer, w1, w2, n_rows):
    tm, th = MLP_TM, MLP_TH
    return pl.pallas_call(
        _mlp_kernel,
        grid=(n_rows // tm, MLP_HIDDEN // th),
        in_specs=[
            pl.BlockSpec((tm, D_MODEL), lambda i, k: (i, 0)),
            pl.BlockSpec((D_MODEL, th), lambda i, k: (0, k)),
            pl.BlockSpec((th, D_MODEL), lambda i, k: (k, 0)),
            pl.BlockSpec((tm, D_MODEL), lambda i, k: (i, 0)),
            pl.BlockSpec((None, None, 1, D_MODEL), lambda i, k: (layer, _mod_row(i, tm), 0, 5)),
        ],
        out_specs=pl.BlockSpec((tm, D_MODEL), lambda i, k: (i, 0)),
        out_shape=jax.ShapeDtypeStruct((n_rows, D_MODEL), F32),
        scratch_shapes=[pltpu.VMEM((tm, D_MODEL), F32)],
        compiler_params=_params(2),
        name="mlp",
    )(h2, w1, w2, x_new, mods)


def _rope_tables(tm):
    t = jnp.arange(SEQ)
    row = (t // GRID_W).astype(F32)
    col = (t % GRID_W).astype(F32)
    inv_freq = ROPE_THETA ** (-jnp.arange(0, AXIS_DIM, 2, dtype=F32) / AXIS_DIM)
    ang_r = row[:, None] * inv_freq[None, :]
    ang_c = col[:, None] * inv_freq[None, :]
    ang = jnp.concatenate([ang_r, ang_r, ang_c, ang_c], axis=-1)
    cos, sin = jnp.cos(ang), jnp.sin(ang)
    first = (jnp.arange(HEAD_DIM) % AXIS_DIM) < AXIS_DIM // 2
    sin_a = jnp.where(first, -sin, 0.0)
    sin_b = jnp.where(first, 0.0, sin)
    ident = jnp.zeros((tm, HEAD_DIM), F32)
    return (jnp.concatenate([cos, ident + 1.0]), jnp.concatenate([sin_a, ident]),
            jnp.concatenate([sin_b, ident]))


def _reorder_w_in(w):
    o_gb = ATTN_WIDTH + 2 * KV_WIDTH
    o_gc = o_gb + CONV_WIDTH
    o_u = o_gc + CONV_WIDTH
    d = w.shape[0]
    gc = w[:, o_gc:o_u].reshape(d, CONV_WIDTH // _ZC, 1, _ZC)
    u = w[:, o_u:].reshape(d, CONV_WIDTH // _ZC, 1, _ZC)
    zcols = jnp.concatenate([gc, u], axis=2).reshape(d, 2 * CONV_WIDTH)
    return jnp.concatenate([w[:, :o_gb], zcols, w[:, o_gb:o_gc]], axis=1).astype(BF16)


def kernel(x, c, ctx, c_ctx, w_ada, b_ada, norm1_g, w_in, q_norm_g, k_norm_g, conv_w, conv_b,
           attn_out_g, conv_out_g, w_out, norm2_g, w_mlp_in, w_mlp_out):
    assert x.shape == (BATCH, SEQ, D_MODEL) and ctx.shape == (BATCH, CTX_LEN, D_MODEL)
    xs = jnp.concatenate([x.reshape(N_LATENT, D_MODEL), ctx.reshape(N_CTX, D_MODEL)], axis=0)
    cond = jnp.concatenate([c, c_ctx[None, :], jnp.zeros((MOD_ROWS - BATCH - 1, D_MODEL), F32)], axis=0)
    mods = _adaln_all(cond, w_ada, b_ada).reshape(DEPTH, MOD_ROWS, 1, N_MOD * D_MODEL)
    cos_t, sa_t, sb_t = _rope_tables(INPROJ_TM)

    for l in range(DEPTH):
        q, kv, z, gb = _in_proj(xs, mods, l, norm1_g[l][None], _reorder_w_in(w_in[l]),
                                q_norm_g[l][None], k_norm_g[l][None], cos_t, sa_t, sb_t)
        attn = _attention(q, kv)
        n_rows = N_TOKENS if l < DEPTH - 1 else N_LATENT
        x_new, h2 = _merge(xs, attn, z, gb, mods, l, conv_w[l], conv_b[l][None], attn_out_g[l][None],
                           conv_out_g[l][None], w_out[l].astype(BF16), norm2_g[l][None], n_rows)
        xs = _mlp(h2, x_new, mods, l, w_mlp_in[l].astype(BF16), w_mlp_out[l].astype(BF16), n_rows)
    return xs.reshape(BATCH, SEQ, D_MODEL)
```

```python
import functools

import jax
import jax.numpy as jnp
from jax import lax
from jax.experimental import pallas as pl
from jax.experimental.pallas import tpu as pltpu

D_MODEL = 2048
BATCH = 4
SEQ = 4096
DEPTH = 4
CTX_LEN = 256
GRID_W = 64
HEAD_DIM = 128
N_HEADS = 8
N_KV_HEADS = 2
Q_PER_KV = N_HEADS // N_KV_HEADS
ATTN_WIDTH = N_HEADS * HEAD_DIM
KV_WIDTH = N_KV_HEADS * HEAD_DIM
CONV_WIDTH = D_MODEL - ATTN_WIDTH
IN_WIDTH = ATTN_WIDTH + 2 * KV_WIDTH + 3 * CONV_WIDTH
MLP_HIDDEN = 4 * D_MODEL
N_MOD = 6
ROPE_THETA = 10000.0
AXIS_DIM = HEAD_DIM // 2
EPS = 1e-6
LOG2_E = 1.4426950408889634
KV_COLS = KV_WIDTH + 2 * KV_WIDTH

N_LATENT = BATCH * SEQ
N_CTX = BATCH * CTX_LEN
N_TOKENS = N_LATENT + N_CTX
MOD_ROWS = 8

F32 = jnp.float32
BF16 = jnp.bfloat16

ADA_TN = 1024
INPROJ_TM = 1024
INPROJ_TN = 512
ATTN_TQ = 256
ATTN_TK = 512
MERGE_TM = 256
MLP_TM = 512
MLP_TH = 512
HALO = 8
VMEM_LIMIT = 56 * 1024 * 1024


def _params(n_axes, vmem=VMEM_LIMIT):
    return pltpu.CompilerParams(dimension_semantics=("arbitrary",) * n_axes, vmem_limit_bytes=vmem)


def _mod_row(tile, tm):
    return jnp.minimum(tile // (SEQ // tm), BATCH)


def _rms(x, g):
    return x * lax.rsqrt(jnp.mean(x * x, axis=-1, keepdims=True) + EPS) * g


def _ada_kernel(cond_ref, w_ref, b_ref, o_ref):
    cnd = cond_ref[...]
    s = (cnd * jax.nn.sigmoid(cnd)).astype(BF16)
    w = w_ref[...].astype(BF16)
    o_ref[...] = jnp.dot(s, w, preferred_element_type=F32) + b_ref[...]


def _adaln_all(cond, w_ada, b_ada):
    n = N_MOD * D_MODEL
    return pl.pallas_call(
        _ada_kernel,
        grid=(DEPTH, n // ADA_TN),
        in_specs=[
            pl.BlockSpec((MOD_ROWS, D_MODEL), lambda l, j: (0, 0)),
            pl.BlockSpec((None, D_MODEL, ADA_TN), lambda l, j: (l, 0, j)),
            pl.BlockSpec((None, 1, ADA_TN), lambda l, j: (l, 0, j)),
        ],
        out_specs=pl.BlockSpec((None, MOD_ROWS, ADA_TN), lambda l, j: (l, 0, j)),
        out_shape=jax.ShapeDtypeStruct((DEPTH, MOD_ROWS, n), F32),
        compiler_params=_params(2),
        name="adaln",
    )(cond, w_ada, b_ada.reshape(DEPTH, 1, n))


_J_KV = ATTN_WIDTH // INPROJ_TN
_J_Z0 = _J_KV + 1
_ZC = INPROJ_TN // 2
_J_GB0 = _J_Z0 + CONV_WIDTH // _ZC
_J_END = _J_GB0 + CONV_WIDTH // INPROJ_TN


def _rope(x, cos, sin_a, sin_b):
    return (x * cos + pltpu.roll(x, HEAD_DIM - AXIS_DIM // 2, axis=1) * sin_a
            + pltpu.roll(x, AXIS_DIM // 2, axis=1) * sin_b)


def _inproj_kernel(x_ref, sh_ref, sc_ref, n1g_ref, w_ref, qg_ref, kg_ref, cos_ref, sa_ref, sb_ref,
                   q_ref, kv_ref, z_ref, gb_ref, h_scr):
    j = pl.program_id(1)

    @pl.when(j == 0)
    def _():
        h = _rms(x_ref[...], n1g_ref[...]) * (1.0 + sc_ref[...]) + sh_ref[...]
        h_scr[...] = h.astype(BF16)

    r = jnp.dot(h_scr[...], w_ref[...], preferred_element_type=F32)

    def heads(r_cols, gain, scale):
        out = []
        for hd in range(r_cols.shape[1] // HEAD_DIM):
            y = _rms(r_cols[:, hd * HEAD_DIM:(hd + 1) * HEAD_DIM], gain)
            y = _rope(y, cos_ref[...], sa_ref[...], sb_ref[...])
            out.append((y * scale).astype(BF16))
        return out

    @pl.when(j < _J_KV)
    def _():
        for hd, y in enumerate(heads(r, qg_ref[...], HEAD_DIM ** -0.5 * LOG2_E)):
            q_ref[:, hd * HEAD_DIM:(hd + 1) * HEAD_DIM] = y

    @pl.when(j == _J_KV)
    def _():
        for hd, y in enumerate(heads(r[:, :KV_WIDTH], kg_ref[...], 1.0)):
            kv_ref[:, hd * HEAD_DIM:(hd + 1) * HEAD_DIM] = y
        for hd in range(N_KV_HEADS):
            v0 = KV_WIDTH + hd * 2 * HEAD_DIM
            kv_ref[:, v0:v0 + HEAD_DIM] = r[:, KV_WIDTH + hd * HEAD_DIM:KV_WIDTH + (hd + 1) * HEAD_DIM].astype(BF16)
            kv_ref[:, v0 + HEAD_DIM:v0 + 2 * HEAD_DIM] = jnp.ones((r.shape[0], HEAD_DIM), BF16)

    @pl.when(jnp.logical_and(j >= _J_Z0, j < _J_GB0))
    def _():
        z_ref[...] = r[:, :_ZC] * r[:, _ZC:]

    @pl.when(j >= _J_GB0)
    def _():
        gb_ref[...] = r


def _in_proj(xs, mods, layer, n1g, w_in_r, qg, kg, cos_t, sa_t, sb_t):
    tm, tn = INPROJ_TM, INPROJ_TN
    n_tiles = N_TOKENS // tm
    lat_tiles = SEQ // tm

    def mod_spec(chunk):
        return pl.BlockSpec((None, None, 1, D_MODEL),
                            lambda i, j: (layer, _mod_row(i, tm), 0, chunk))

    def tab_spec():
        return pl.BlockSpec((tm, HEAD_DIM),
                            lambda i, j: (jnp.where(i < N_LATENT // tm, i % lat_tiles, lat_tiles), 0))

    return pl.pallas_call(
        _inproj_kernel,
        grid=(n_tiles, _J_END),
        in_specs=[
            pl.BlockSpec((tm, D_MODEL), lambda i, j: (i, 0)),
            mod_spec(0), mod_spec(1),
            pl.BlockSpec((1, D_MODEL), lambda i, j: (0, 0)),
            pl.BlockSpec((D_MODEL, tn), lambda i, j: (0, j)),
            pl.BlockSpec((1, HEAD_DIM), lambda i, j: (0, 0)),
            pl.BlockSpec((1, HEAD_DIM), lambda i, j: (0, 0)),
            tab_spec(), tab_spec(), tab_spec(),
        ],
        out_specs=[
            pl.BlockSpec((tm, tn), lambda i, j: (i, jnp.minimum(j, _J_KV - 1))),
            pl.BlockSpec((tm, KV_COLS), lambda i, j: (i, 0)),
            pl.BlockSpec((tm, _ZC), lambda i, j: (i, jnp.clip(j - _J_Z0, 0, CONV_WIDTH // _ZC - 1))),
            pl.BlockSpec((tm, tn), lambda i, j: (i, jnp.clip(j - _J_GB0, 0, CONV_WIDTH // tn - 1))),
        ],
        out_shape=[
            jax.ShapeDtypeStruct((N_TOKENS, ATTN_WIDTH), BF16),
            jax.ShapeDtypeStruct((N_TOKENS, KV_COLS), BF16),
            jax.ShapeDtypeStruct((N_TOKENS, CONV_WIDTH), F32),
            jax.ShapeDtypeStruct((N_TOKENS, CONV_WIDTH), F32),
        ],
        scratch_shapes=[pltpu.VMEM((tm, D_MODEL), BF16)],
        compiler_params=_params(2),
        name="in_proj",
    )(xs, mods, mods, n1g, w_in_r, qg, kg, cos_t, sa_t, sb_t)


_NQ = SEQ // ATTN_TQ


def _attend(q, chunks):
    def scores(chunk):
        k_ref, _, start, size = chunk
        return lax.dot_general(q, k_ref[start:start + size, :], (((1,), (1,)), ((), ())),
                               preferred_element_type=F32)

    m = acc = None
    s = scores(chunks[0])
    for idx, (_, v_ref, start, size) in enumerate(chunks):
        s_next = scores(chunks[idx + 1]) if idx + 1 < len(chunks) else None
        m_c = jnp.max(s, axis=-1, keepdims=True)
        m_new = m_c if m is None else jnp.maximum(m, m_c)
        p = jnp.exp2(s - m_new).astype(BF16)
        pv = jnp.dot(p, v_ref[start:start + size, :], preferred_element_type=F32)
        acc = pv if m is None else jnp.exp2(m - m_new) * acc + pv
        m, s = m_new, s_next
    return acc[:, :HEAD_DIM] / acc[:, HEAD_DIM:]


def _attn_kernel(q_ref, kl_ref, vl_ref, kc_ref, vc_ref, o_ref):
    qi = pl.program_id(2)
    tq = q_ref.shape[0]
    ctx_chunk = [(kc_ref, vc_ref, 0, CTX_LEN)]
    lat_chunks = [(kl_ref, vl_ref, c * ATTN_TK, ATTN_TK) for c in range(SEQ // ATTN_TK)]

    def run(chunks):
        q = jnp.concatenate([q_ref[:, r * HEAD_DIM:(r + 1) * HEAD_DIM] for r in range(Q_PER_KV)], axis=0)
        o = _attend(q, chunks).astype(BF16)
        for r in range(Q_PER_KV):
            o_ref[:, r * HEAD_DIM:(r + 1) * HEAD_DIM] = o[r * tq:(r + 1) * tq, :]

    @pl.when(qi < _NQ)
    def _():
        run(lat_chunks + ctx_chunk)

    @pl.when(qi == _NQ)
    def _():
        run(ctx_chunk)


def _attention(q, kv):
    tq = ATTN_TQ
    ctx_blk0 = N_LATENT // CTX_LEN
    v_blk0 = KV_WIDTH // (2 * HEAD_DIM)

    def q_map(b, g, qi):
        return (jnp.where(qi < _NQ, b * _NQ + qi, N_LATENT // tq + b), g)

    return pl.pallas_call(
        _attn_kernel,
        grid=(BATCH, N_KV_HEADS, _NQ + 1),
        in_specs=[
            pl.BlockSpec((tq, Q_PER_KV * HEAD_DIM), q_map),
            pl.BlockSpec((SEQ, HEAD_DIM), lambda b, g, qi: (b, g)),
            pl.BlockSpec((SEQ, 2 * HEAD_DIM), lambda b, g, qi: (b, v_blk0 + g)),
            pl.BlockSpec((CTX_LEN, HEAD_DIM), lambda b, g, qi: (ctx_blk0 + b, g)),
            pl.BlockSpec((CTX_LEN, 2 * HEAD_DIM), lambda b, g, qi: (ctx_blk0 + b, v_blk0 + g)),
        ],
        out_specs=pl.BlockSpec((tq, Q_PER_KV * HEAD_DIM), q_map),
        out_shape=jax.ShapeDtypeStruct((N_TOKENS, ATTN_WIDTH), BF16),
        compiler_params=_params(3),
        name="attention",
    )(q, kv, kv, kv, kv)


def _merge_kernel(x_ref, attn_ref, z_ref, zp_ref, zn_ref, gb_ref, cw_ref, cb_ref, ag_ref, cg_ref, w_ref,
                  g1_ref, sh2_ref, sc2_ref, n2g_ref, xo_ref, h2_ref, cat_scr):
    tm = x_ref.shape[0]
    i = pl.program_id(0)

    row = lax.broadcasted_iota(jnp.int32, (tm, 1), 0)
    grow = i * tm + row
    seg = jnp.where(grow >= N_LATENT, CTX_LEN, SEQ)
    pos = jnp.bitwise_and(grow, seg - 1)

    z = z_ref[...]
    z_prev = jnp.where(row == 0, zp_ref[HALO - 1:HALO, :], pltpu.roll(z, 1, axis=0))
    z_prev = jnp.where(pos != 0, z_prev, 0.0)
    z_next = jnp.where(row == tm - 1, zn_ref[0:1, :], pltpu.roll(z, tm - 1, axis=0))
    z_next = jnp.where(pos != seg - 1, z_next, 0.0)
    conv = z_prev * cw_ref[0:1, :] + z * cw_ref[1:2, :] + z_next * cw_ref[2:3, :] + cb_ref[...]
    conv = gb_ref[...] * conv

    cat_scr[:, :ATTN_WIDTH] = _rms(attn_ref[...].astype(F32), ag_ref[...]).astype(BF16)
    cat_scr[:, ATTN_WIDTH:] = _rms(conv, cg_ref[...]).astype(BF16)
    y = jnp.dot(cat_scr[...], w_ref[...], preferred_element_type=F32)
    x_new = x_ref[...] + g1_ref[...] * y
    xo_ref[...] = x_new
    h2 = _rms(x_new, n2g_ref[...]) * (1.0 + sc2_ref[...]) + sh2_ref[...]
    h2_ref[...] = h2.astype(BF16)


def _merge(xs, attn, z, gb, mods, layer, cw, cb, ag, cg, w_out, n2g, n_rows):
    tm = MERGE_TM
    hb = tm // HALO
    last_hb = N_TOKENS // HALO - 1

    def mod_spec(chunk):
        return pl.BlockSpec((None, None, 1, D_MODEL), lambda i: (layer, _mod_row(i, tm), 0, chunk))

    def vec_spec(rows, width):
        return pl.BlockSpec((rows, width), lambda i: (0, 0))

    return pl.pallas_call(
        _merge_kernel,
        grid=(n_rows // tm,),
        in_specs=[
            pl.BlockSpec((tm, D_MODEL), lambda i: (i, 0)),
            pl.BlockSpec((tm, ATTN_WIDTH), lambda i: (i, 0)),
            pl.BlockSpec((tm, CONV_WIDTH), lambda i: (i, 0)),
            pl.BlockSpec((HALO, CONV_WIDTH), lambda i: (jnp.maximum(i * hb - 1, 0), 0)),
            pl.BlockSpec((HALO, CONV_WIDTH), lambda i: (jnp.minimum((i + 1) * hb, last_hb), 0)),
            pl.BlockSpec((tm, CONV_WIDTH), lambda i: (i, 0)),
            vec_spec(3, CONV_WIDTH), vec_spec(1, CONV_WIDTH),
            vec_spec(1, ATTN_WIDTH), vec_spec(1, CONV_WIDTH),
            pl.BlockSpec((D_MODEL, D_MODEL), lambda i: (0, 0)),
            mod_spec(2), mod_spec(3), mod_spec(4),
            vec_spec(1, D_MODEL),
        ],
        out_specs=[
            pl.BlockSpec((tm, D_MODEL), lambda i: (i, 0)),
            pl.BlockSpec((tm, D_MODEL), lambda i: (i, 0)),
        ],
        out_shape=[
            jax.ShapeDtypeStruct((n_rows, D_MODEL), F32),
            jax.ShapeDtypeStruct((n_rows, D_MODEL), BF16),
        ],
        scratch_shapes=[pltpu.VMEM((tm, D_MODEL), BF16)],
        compiler_params=_params(1),
        name="merge_out_proj",
    )(xs, attn, z, z, z, gb, cw, cb, ag, cg, w_out, mods, mods, mods, n2g)


def _mlp_kernel(h_ref, w1_ref, w2_ref, x_ref, g2_ref, o_ref, acc_scr):
    k = pl.program_id(1)
    a = jnp.maximum(jnp.dot(h_ref[...], w1_ref[...], preferred_element_type=F32), 0.0)
    part = jnp.dot((a * a).astype(BF16), w2_ref[...], preferred_element_type=F32)

    @pl.when(k == 0)
    def _():
        acc_scr[...] = part

    @pl.when(k > 0)
    def _():
        acc_scr[...] += part

    @pl.when(k == pl.num_programs(1) - 1)
    def _():
        o_ref[...] = x_ref[...] + g2_ref[...] * acc_scr[...]


def _mlp(h2, x_new, mods, layer, w1, w2, n_rows):
    tm, th = MLP_TM, MLP_TH
    return pl.pallas_call(
        _mlp_kernel,
        grid=(n_rows // tm, MLP_HIDDEN // th),
        in_specs=[
            pl.BlockSpec((tm, D_MODEL), lambda i, k: (i, 0)),
            pl.BlockSpec((D_MODEL, th), lambda i, k: (0, k)),
            pl.BlockSpec((th, D_MODEL), lambda i, k: (k, 0)),
            pl.BlockSpec((tm, D_MODEL), lambda i, k: (i, 0)),
            pl.BlockSpec((None, None, 1, D_MODEL), lambda i, k: (layer, _mod_row(i, tm), 0, 5)),
        ],
        out_specs=pl.BlockSpec((tm, D_MODEL), lambda i, k: (i, 0)),
        out_shape=jax.ShapeDtypeStruct((n_rows, D_MODEL), F32),
        scratch_shapes=[pltpu.VMEM((tm, D_MODEL), F32)],
        compiler_params=_params(2),
        name="mlp",
    )(h2, w1, w2, x_new, mods)


def _rope_tables(tm):
    t = jnp.arange(SEQ)
    row = (t // GRID_W).astype(F32)
    col = (t % GRID_W).astype(F32)
    inv_freq = ROPE_THETA ** (-jnp.arange(0, AXIS_DIM, 2, dtype=F32) / AXIS_DIM)
    ang_r = row[:, None] * inv_freq[None, :]
    ang_c = col[:, None] * inv_freq[None, :]
    ang = jnp.concatenate([ang_r, ang_r, ang_c, ang_c], axis=-1)
    cos, sin = jnp.cos(ang), jnp.sin(ang)
    first = (jnp.arange(HEAD_DIM) % AXIS_DIM) < AXIS_DIM // 2
    sin_a = jnp.where(first, -sin, 0.0)
    sin_b = jnp.where(first, 0.0, sin)
    ident = jnp.zeros((tm, HEAD_DIM), F32)
    return (jnp.concatenate([cos, ident + 1.0]), jnp.concatenate([sin_a, ident]),
            jnp.concatenate([sin_b, ident]))


def _reorder_w_in(w):
    o_gb = ATTN_WIDTH + 2 * KV_WIDTH
    o_gc = o_gb + CONV_WIDTH
    o_u = o_gc + CONV_WIDTH
    d = w.shape[0]
    gc = w[:, o_gc:o_u].reshape(d, CONV_WIDTH // _ZC, 1, _ZC)
    u = w[:, o_u:].reshape(d, CONV_WIDTH // _ZC, 1, _ZC)
    zcols = jnp.concatenate([gc, u], axis=2).reshape(d, 2 * CONV_WIDTH)
    return jnp.concatenate([w[:, :o_gb], zcols, w[:, o_gb:o_gc]], axis=1).astype(BF16)


def kernel(x, c, ctx, c_ctx, w_ada, b_ada, norm1_g, w_in, q_norm_g, k_norm_g, conv_w, conv_b,
           attn_out_g, conv_out_g, w_out, norm2_g, w_mlp_in, w_mlp_out):
    assert x.shape == (BATCH, SEQ, D_MODEL) and ctx.shape == (BATCH, CTX_LEN, D_MODEL)
    xs = jnp.concatenate([x.reshape(N_LATENT, D_MODEL), ctx.reshape(N_CTX, D_MODEL)], axis=0)
    cond = jnp.concatenate([c, c_ctx[None, :], jnp.zeros((MOD_ROWS - BATCH - 1, D_MODEL), F32)], axis=0)
    mods = _adaln_all(cond, w_ada, b_ada).reshape(DEPTH, MOD_ROWS, 1, N_MOD * D_MODEL)
    cos_t, sa_t, sb_t = _rope_tables(INPROJ_TM)

    for l in range(DEPTH):
        q, kv, z, gb = _in_proj(xs, mods, l, norm1_g[l][None], _reorder_w_in(w_in[l]),
                                q_norm_g[l][None], k_norm_g[l][None], cos_t, sa_t, sb_t)
        attn = _attention(q, kv)
        n_rows = N_TOKENS if l < DEPTH - 1 else N_LATENT
        x_new, h2 = _merge(xs, attn, z, gb, mods, l, conv_w[l], conv_b[l][None], attn_out_g[l][None],
                           conv_out_g[l][None], w_out[l].astype(BF16), norm2_g[l][None], n_rows)
        xs = _mlp(h2, x_new, mods, l, w_mlp_in[l].astype(BF16), w_mlp_out[l].astype(BF16), n_rows)
    return xs.reshape(BATCH, SEQ, D_MODEL)
```

```python
import functools

import jax
import jax.numpy as jnp
from jax import lax
from jax.experimental import pallas as pl
from jax.experimental.pallas import tpu as pltpu

D_MODEL = 2048
BATCH = 4
SEQ = 4096
DEPTH = 4
CTX_LEN = 256
GRID_W = 64
HEAD_DIM = 128
N_HEADS = 8
N_KV_HEADS = 2
Q_PER_KV = N_HEADS // N_KV_HEADS
ATTN_WIDTH = N_HEADS * HEAD_DIM
KV_WIDTH = N_KV_HEADS * HEAD_DIM
CONV_WIDTH = D_MODEL - ATTN_WIDTH
IN_WIDTH = ATTN_WIDTH + 2 * KV_WIDTH + 3 * CONV_WIDTH
MLP_HIDDEN = 4 * D_MODEL
N_MOD = 6
ROPE_THETA = 10000.0
AXIS_DIM = HEAD_DIM // 2
EPS = 1e-6
LOG2_E = 1.4426950408889634
KV_COLS = KV_WIDTH + 2 * KV_WIDTH

N_LATENT = BATCH * SEQ
N_CTX = BATCH * CTX_LEN
N_TOKENS = N_LATENT + N_CTX
MOD_ROWS = 8

F32 = jnp.float32
BF16 = jnp.bfloat16

ADA_TN = 1024
INPROJ_TM = 1024
INPROJ_TN = 512
ATTN_TQ = 256
ATTN_TK = 512
MERGE_TM = 256
MLP_TM = 512
MLP_TH = 1024
MLP_SUB = 512
HALO = 8
VMEM_LIMIT = 56 * 1024 * 1024


def _params(n_axes, vmem=VMEM_LIMIT):
    return pltpu.CompilerParams(dimension_semantics=("arbitrary",) * n_axes, vmem_limit_bytes=vmem)


def _mod_row(tile, tm):
    return jnp.minimum(tile // (SEQ // tm), BATCH)


def _rms(x, g):
    return x * lax.rsqrt(jnp.mean(x * x, axis=-1, keepdims=True) + EPS) * g


def _ada_kernel(cond_ref, w_ref, b_ref, o_ref):
    cnd = cond_ref[...]
    s = (cnd * jax.nn.sigmoid(cnd)).astype(BF16)
    w = w_ref[...].astype(BF16)
    o_ref[...] = jnp.dot(s, w, preferred_element_type=F32) + b_ref[...]


def _adaln_all(cond, w_ada, b_ada):
    n = N_MOD * D_MODEL
    return pl.pallas_call(
        _ada_kernel,
        grid=(DEPTH, n // ADA_TN),
        in_specs=[
            pl.BlockSpec((MOD_ROWS, D_MODEL), lambda l, j: (0, 0)),
            pl.BlockSpec((None, D_MODEL, ADA_TN), lambda l, j: (l, 0, j)),
            pl.BlockSpec((None, 1, ADA_TN), lambda l, j: (l, 0, j)),
        ],
        out_specs=pl.BlockSpec((None, MOD_ROWS, ADA_TN), lambda l, j: (l, 0, j)),
        out_shape=jax.ShapeDtypeStruct((DEPTH, MOD_ROWS, n), F32),
        compiler_params=_params(2),
        name="adaln",
    )(cond, w_ada, b_ada.reshape(DEPTH, 1, n))


_J_KV = ATTN_WIDTH // INPROJ_TN
_J_Z0 = _J_KV + 1
_ZC = INPROJ_TN // 2
_J_GB0 = _J_Z0 + CONV_WIDTH // _ZC
_J_END = _J_GB0 + CONV_WIDTH // INPROJ_TN


def _rope(x, cos, sin_a, sin_b):
    return (x * cos + pltpu.roll(x, HEAD_DIM - AXIS_DIM // 2, axis=1) * sin_a
            + pltpu.roll(x, AXIS_DIM // 2, axis=1) * sin_b)


def _inproj_kernel(x_ref, sh_ref, sc_ref, n1g_ref, w_ref, qg_ref, kg_ref, cos_ref, sa_ref, sb_ref,
                   q_ref, kv_ref, z_ref, gb_ref, h_scr):
    j = pl.program_id(1)

    @pl.when(j == 0)
    def _():
        h = _rms(x_ref[...], n1g_ref[...]) * (1.0 + sc_ref[...]) + sh_ref[...]
        h_scr[...] = h.astype(BF16)

    r = jnp.dot(h_scr[...], w_ref[...], preferred_element_type=F32)

    def heads(r_cols, gain, scale):
        out = []
        for hd in range(r_cols.shape[1] // HEAD_DIM):
            y = _rms(r_cols[:, hd * HEAD_DIM:(hd + 1) * HEAD_DIM], gain)
            y = _rope(y, cos_ref[...], sa_ref[...], sb_ref[...])
            out.append((y * scale).astype(BF16))
        return out

    @pl.when(j < _J_KV)
    def _():
        for hd, y in enumerate(heads(r, qg_ref[...], HEAD_DIM ** -0.5 * LOG2_E)):
            q_ref[:, hd * HEAD_DIM:(hd + 1) * HEAD_DIM] = y

    @pl.when(j == _J_KV)
    def _():
        for hd, y in enumerate(heads(r[:, :KV_WIDTH], kg_ref[...], 1.0)):
            kv_ref[:, hd * HEAD_DIM:(hd + 1) * HEAD_DIM] = y
        for hd in range(N_KV_HEADS):
            v0 = KV_WIDTH + hd * 2 * HEAD_DIM
            kv_ref[:, v0:v0 + HEAD_DIM] = r[:, KV_WIDTH + hd * HEAD_DIM:KV_WIDTH + (hd + 1) * HEAD_DIM].astype(BF16)
            kv_ref[:, v0 + HEAD_DIM:v0 + 2 * HEAD_DIM] = jnp.ones((r.shape[0], HEAD_DIM), BF16)

    @pl.when(jnp.logical_and(j >= _J_Z0, j < _J_GB0))
    def _():
        z_ref[...] = r[:, :_ZC] * r[:, _ZC:]

    @pl.when(j >= _J_GB0)
    def _():
        gb_ref[...] = r


def _in_proj(xs, mods, layer, n1g, w_in_r, qg, kg, cos_t, sa_t, sb_t):
    tm, tn = INPROJ_TM, INPROJ_TN
    n_tiles = N_TOKENS // tm
    lat_tiles = SEQ // tm

    def mod_spec(chunk):
        return pl.BlockSpec((None, None, 1, D_MODEL),
                            lambda i, j: (layer, _mod_row(i, tm), 0, chunk))

    def tab_spec():
        return pl.BlockSpec((tm, HEAD_DIM),
                            lambda i, j: (jnp.where(i < N_LATENT // tm, i % lat_tiles, lat_tiles), 0))

    return pl.pallas_call(
        _inproj_kernel,
        grid=(n_tiles, _J_END),
        in_specs=[
            pl.BlockSpec((tm, D_MODEL), lambda i, j: (i, 0)),
            mod_spec(0), mod_spec(1),
            pl.BlockSpec((1, D_MODEL), lambda i, j: (0, 0)),
            pl.BlockSpec((D_MODEL, tn), lambda i, j: (0, j)),
            pl.BlockSpec((1, HEAD_DIM), lambda i, j: (0, 0)),
            pl.BlockSpec((1, HEAD_DIM), lambda i, j: (0, 0)),
            tab_spec(), tab_spec(), tab_spec(),
        ],
        out_specs=[
            pl.BlockSpec((tm, tn), lambda i, j: (i, jnp.minimum(j, _J_KV - 1))),
            pl.BlockSpec((tm, KV_COLS), lambda i, j: (i, 0)),
            pl.BlockSpec((tm, _ZC), lambda i, j: (i, jnp.clip(j - _J_Z0, 0, CONV_WIDTH // _ZC - 1))),
            pl.BlockSpec((tm, tn), lambda i, j: (i, jnp.clip(j - _J_GB0, 0, CONV_WIDTH // tn - 1))),
        ],
        out_shape=[
            jax.ShapeDtypeStruct((N_TOKENS, ATTN_WIDTH), BF16),
            jax.ShapeDtypeStruct((N_TOKENS, KV_COLS), BF16),
            jax.ShapeDtypeStruct((N_TOKENS, CONV_WIDTH), F32),
            jax.ShapeDtypeStruct((N_TOKENS, CONV_WIDTH), F32),
        ],
        scratch_shapes=[pltpu.VMEM((tm, D_MODEL), BF16)],
        compiler_params=_params(2),
        name="in_proj",
    )(xs, mods, mods, n1g, w_in_r, qg, kg, cos_t, sa_t, sb_t)


_NQ = SEQ // ATTN_TQ


def _attend(q, chunks):
    def scores(chunk):
        k_ref, _, start, size = chunk
        return lax.dot_general(q, k_ref[start:start + size, :], (((1,), (1,)), ((), ())),
                               preferred_element_type=F32)

    m = acc = None
    s = scores(chunks[0])
    for idx, (_, v_ref, start, size) in enumerate(chunks):
        s_next = scores(chunks[idx + 1]) if idx + 1 < len(chunks) else None
        m_c = jnp.max(s, axis=-1, keepdims=True)
        m_new = m_c if m is None else jnp.maximum(m, m_c)
        p = jnp.exp2(s - m_new).astype(BF16)
        pv = jnp.dot(p, v_ref[start:start + size, :], preferred_element_type=F32)
        acc = pv if m is None else jnp.exp2(m - m_new) * acc + pv
        m, s = m_new, s_next
    return acc[:, :HEAD_DIM] / acc[:, HEAD_DIM:]


def _attn_kernel(q_ref, kl_ref, vl_ref, kc_ref, vc_ref, o_ref):
    qi = pl.program_id(2)
    tq = q_ref.shape[0]
    ctx_chunk = [(kc_ref, vc_ref, 0, CTX_LEN)]
    lat_chunks = [(kl_ref, vl_ref, c * ATTN_TK, ATTN_TK) for c in range(SEQ // ATTN_TK)]

    def run(chunks):
        q = jnp.concatenate([q_ref[:, r * HEAD_DIM:(r + 1) * HEAD_DIM] for r in range(Q_PER_KV)], axis=0)
        o = _attend(q, chunks).astype(BF16)
        for r in range(Q_PER_KV):
            o_ref[:, r * HEAD_DIM:(r + 1) * HEAD_DIM] = o[r * tq:(r + 1) * tq, :]

    @pl.when(qi < _NQ)
    def _():
        run(lat_chunks + ctx_chunk)

    @pl.when(qi == _NQ)
    def _():
        run(ctx_chunk)


def _attention(q, kv):
    tq = ATTN_TQ
    ctx_blk0 = N_LATENT // CTX_LEN
    v_blk0 = KV_WIDTH // (2 * HEAD_DIM)

    def q_map(b, g, qi):
        return (jnp.where(qi < _NQ, b * _NQ + qi, N_LATENT // tq + b), g)

    return pl.pallas_call(
        _attn_kernel,
        grid=(BATCH, N_KV_HEADS, _NQ + 1),
        in_specs=[
            pl.BlockSpec((tq, Q_PER_KV * HEAD_DIM), q_map),
            pl.BlockSpec((SEQ, HEAD_DIM), lambda b, g, qi: (b, g)),
            pl.BlockSpec((SEQ, 2 * HEAD_DIM), lambda b, g, qi: (b, v_blk0 + g)),
            pl.BlockSpec((CTX_LEN, HEAD_DIM), lambda b, g, qi: (ctx_blk0 + b, g)),
            pl.BlockSpec((CTX_LEN, 2 * HEAD_DIM), lambda b, g, qi: (ctx_blk0 + b, v_blk0 + g)),
        ],
        out_specs=pl.BlockSpec((tq, Q_PER_KV * HEAD_DIM), q_map),
        out_shape=jax.ShapeDtypeStruct((N_TOKENS, ATTN_WIDTH), BF16),
        compiler_params=_params(3),
        name="attention",
    )(q, kv, kv, kv, kv)


def _merge_kernel(x_ref, attn_ref, z_ref, zp_ref, zn_ref, gb_ref, cw_ref, cb_ref, ag_ref, cg_ref, w_ref,
                  g1_ref, sh2_ref, sc2_ref, n2g_ref, xo_ref, h2_ref, cat_scr):
    tm = x_ref.shape[0]
    i = pl.program_id(0)

    row = lax.broadcasted_iota(jnp.int32, (tm, 1), 0)
    grow = i * tm + row
    seg = jnp.where(grow >= N_LATENT, CTX_LEN, SEQ)
    pos = jnp.bitwise_and(grow, seg - 1)

    z = z_ref[...]
    z_prev = jnp.where(row == 0, zp_ref[HALO - 1:HALO, :], pltpu.roll(z, 1, axis=0))
    z_prev = jnp.where(pos != 0, z_prev, 0.0)
    z_next = jnp.where(row == tm - 1, zn_ref[0:1, :], pltpu.roll(z, tm - 1, axis=0))
    z_next = jnp.where(pos != seg - 1, z_next, 0.0)
    conv = z_prev * cw_ref[0:1, :] + z * cw_ref[1:2, :] + z_next * cw_ref[2:3, :] + cb_ref[...]
    conv = gb_ref[...] * conv

    cat_scr[:, :ATTN_WIDTH] = _rms(attn_ref[...].astype(F32), ag_ref[...]).astype(BF16)
    cat_scr[:, ATTN_WIDTH:] = _rms(conv, cg_ref[...]).astype(BF16)
    y = jnp.dot(cat_scr[...], w_ref[...], preferred_element_type=F32)
    x_new = x_ref[...] + g1_ref[...] * y
    xo_ref[...] = x_new
    h2 = _rms(x_new, n2g_ref[...]) * (1.0 + sc2_ref[...]) + sh2_ref[...]
    h2_ref[...] = h2.astype(BF16)


def _merge(xs, attn, z, gb, mods, layer, cw, cb, ag, cg, w_out, n2g, n_rows):
    tm = MERGE_TM
    hb = tm // HALO
    last_hb = N_TOKENS // HALO - 1

    def mod_spec(chunk):
        return pl.BlockSpec((None, None, 1, D_MODEL), lambda i: (layer, _mod_row(i, tm), 0, chunk))

    def vec_spec(rows, width):
        return pl.BlockSpec((rows, width), lambda i: (0, 0))

    return pl.pallas_call(
        _merge_kernel,
        grid=(n_rows // tm,),
        in_specs=[
            pl.BlockSpec((tm, D_MODEL), lambda i: (i, 0)),
            pl.BlockSpec((tm, ATTN_WIDTH), lambda i: (i, 0)),
            pl.BlockSpec((tm, CONV_WIDTH), lambda i: (i, 0)),
            pl.BlockSpec((HALO, CONV_WIDTH), lambda i: (jnp.maximum(i * hb - 1, 0), 0)),
            pl.BlockSpec((HALO, CONV_WIDTH), lambda i: (jnp.minimum((i + 1) * hb, last_hb), 0)),
            pl.BlockSpec((tm, CONV_WIDTH), lambda i: (i, 0)),
            vec_spec(3, CONV_WIDTH), vec_spec(1, CONV_WIDTH),
            vec_spec(1, ATTN_WIDTH), vec_spec(1, CONV_WIDTH),
            pl.BlockSpec((D_MODEL, D_MODEL), lambda i: (0, 0)),
            mod_spec(2), mod_spec(3), mod_spec(4),
            vec_spec(1, D_MODEL),
        ],
        out_specs=[
            pl.BlockSpec((tm, D_MODEL), lambda i: (i, 0)),
            pl.BlockSpec((tm, D_MODEL), lambda i: (i, 0)),
        ],
        out_shape=[
            jax.ShapeDtypeStruct((n_rows, D_MODEL), F32),
            jax.ShapeDtypeStruct((n_rows, D_MODEL), BF16),
        ],
        scratch_shapes=[pltpu.VMEM((tm, D_MODEL), BF16)],
        compiler_params=_params(1),
        name="merge_out_proj",
    )(xs, attn, z, z, z, gb, cw, cb, ag, cg, w_out, mods, mods, mods, n2g)


def _mlp_kernel(h_ref, w1_ref, w2_ref, x_ref, g2_ref, o_ref, acc_scr):
    k = pl.program_id(1)

    @pl.when(k == 0)
    def _():
        acc_scr[...] = jnp.zeros_like(acc_scr)

    def up(s):
        return jnp.dot(h_ref[...], w1_ref[:, s * MLP_SUB:(s + 1) * MLP_SUB], preferred_element_type=F32)

    n_sub = w1_ref.shape[1] // MLP_SUB
    part = None
    a_next = up(0)
    for s in range(n_sub):
        a = a_next
        if s + 1 < n_sub:
            a_next = up(s + 1)
        a = jnp.maximum(a, 0.0)
        d = jnp.dot((a * a).astype(BF16), w2_ref[s * MLP_SUB:(s + 1) * MLP_SUB, :], preferred_element_type=F32)
        part = d if part is None else part + d
    acc_scr[...] += part

    @pl.when(k == pl.num_programs(1) - 1)
    def _():
        o_ref[...] = x_ref[...] + g2_ref[...] * acc_scr[...]


def _mlp(h2, x_new, mods, layer, w1, w2, n_rows):
    tm, th = MLP_TM, MLP_TH
    return pl.pallas_call(
        _mlp_kernel,
        grid=(n_rows // tm, MLP_HIDDEN // th),
        in_specs=[
            pl.BlockSpec((tm, D_MODEL), lambda i, k: (i, 0)),
            pl.BlockSpec((D_MODEL, th), lambda i, k: (0, k)),
            pl.BlockSpec((th, D_MODEL), lambda i, k: (k, 0)),
            pl.BlockSpec((tm, D_MODEL), lambda i, k: (i, 0)),
            pl.BlockSpec((None, None, 1, D_MODEL), lambda i, k: (layer, _mod_row(i, tm), 0, 5)),
        ],
        out_specs=pl.BlockSpec((tm, D_MODEL), lambda i, k: (i, 0)),
        out_shape=jax.ShapeDtypeStruct((n_rows, D_MODEL), F32),
        scratch_shapes=[pltpu.VMEM((tm, D_MODEL), F32)],
        compiler_params=_params(2),
        name="mlp",
    )(h2, w1, w2, x_new, mods)


def _rope_tables(tm):
    t = jnp.arange(SEQ)
    row = (t // GRID_W).astype(F32)
    col = (t % GRID_W).astype(F32)
    inv_freq = ROPE_THETA ** (-jnp.arange(0, AXIS_DIM, 2, dtype=F32) / AXIS_DIM)
    ang_r = row[:, None] * inv_freq[None, :]
    ang_c = col[:, None] * inv_freq[None, :]
    ang = jnp.concatenate([ang_r, ang_r, ang_c, ang_c], axis=-1)
    cos, sin = jnp.cos(ang), jnp.sin(ang)
    first = (jnp.arange(HEAD_DIM) % AXIS_DIM) < AXIS_DIM // 2
    sin_a = jnp.where(first, -sin, 0.0)
    sin_b = jnp.where(first, 0.0, sin)
    ident = jnp.zeros((tm, HEAD_DIM), F32)
    return (jnp.concatenate([cos, ident + 1.0]), jnp.concatenate([sin_a, ident]),
            jnp.concatenate([sin_b, ident]))


def _reorder_w_in(w):
    o_gb = ATTN_WIDTH + 2 * KV_WIDTH
    o_gc = o_gb + CONV_WIDTH
    o_u = o_gc + CONV_WIDTH
    d = w.shape[0]
    gc = w[:, o_gc:o_u].reshape(d, CONV_WIDTH // _ZC, 1, _ZC)
    u = w[:, o_u:].reshape(d, CONV_WIDTH // _ZC, 1, _ZC)
    zcols = jnp.concatenate([gc, u], axis=2).reshape(d, 2 * CONV_WIDTH)
    return jnp.concatenate([w[:, :o_gb], zcols, w[:, o_gb:o_gc]], axis=1).astype(BF16)


def kernel(x, c, ctx, c_ctx, w_ada, b_ada, norm1_g, w_in, q_norm_g, k_norm_g, conv_w, conv_b,
           attn_out_g, conv_out_g, w_out, norm2_g, w_mlp_in, w_mlp_out):
    assert x.shape == (BATCH, SEQ, D_MODEL) and ctx.shape == (BATCH, CTX_LEN, D_MODEL)
    xs = jnp.concatenate([x.reshape(N_LATENT, D_MODEL), ctx.reshape(N_CTX, D_MODEL)], axis=0)
    cond = jnp.concatenate([c, c_ctx[None, :], jnp.zeros((MOD_ROWS - BATCH - 1, D_MODEL), F32)], axis=0)
    mods = _adaln_all(cond, w_ada, b_ada).reshape(DEPTH, MOD_ROWS, 1, N_MOD * D_MODEL)
    cos_t, sa_t, sb_t = _rope_tables(INPROJ_TM)

    for l in range(DEPTH):
        q, kv, z, gb = _in_proj(xs, mods, l, norm1_g[l][None], _reorder_w_in(w_in[l]),
                                q_norm_g[l][None], k_norm_g[l][None], cos_t, sa_t, sb_t)
        attn = _attention(q, kv)
        n_rows = N_TOKENS if l < DEPTH - 1 else N_LATENT
        x_new, h2 = _merge(xs, attn, z, gb, mods, l, conv_w[l], conv_b[l][None], attn_out_g[l][None],
                           conv_out_g[l][None], w_out[l].astype(BF16), norm2_g[l][None], n_rows)
        xs = _mlp(h2, x_new, mods, l, w_mlp_in[l].astype(BF16), w_mlp_out[l].astype(BF16), n_rows)
    return xs.reshape(BATCH, SEQ, D_MODEL)
```

```python
import jax
import jax.numpy as jnp
from jax import lax
from jax.experimental import pallas as pl
from jax.experimental.pallas import tpu as pltpu

D_MODEL = 2048
BATCH = 4
SEQ = 4096
DEPTH = 4
CTX_LEN = 256
GRID_W = 64
HEAD_DIM = 128
N_HEADS = 8
N_KV_HEADS = 2
Q_PER_KV = N_HEADS // N_KV_HEADS
ATTN_WIDTH = N_HEADS * HEAD_DIM
KV_WIDTH = N_KV_HEADS * HEAD_DIM
CONV_WIDTH = D_MODEL - ATTN_WIDTH
IN_WIDTH = ATTN_WIDTH + 2 * KV_WIDTH + 3 * CONV_WIDTH
MLP_HIDDEN = 4 * D_MODEL
N_MOD = 6
ROPE_THETA = 10000.0
AXIS_DIM = HEAD_DIM // 2
EPS = 1e-6
LOG2_E = 1.4426950408889634
KV_COLS = KV_WIDTH + 2 * KV_WIDTH

N_LATENT = BATCH * SEQ
N_CTX = BATCH * CTX_LEN
N_TOKENS = N_LATENT + N_CTX
MOD_ROWS = 8

F32 = jnp.float32
BF16 = jnp.bfloat16

ADA_TN = 1024
INPROJ_TM = 1024
INPROJ_TN = 512
ATTN_TQ = 256
ATTN_TK = 512
MERGE_TM = 256
MLP_TM = 512
MLP_TH = 1024
MLP_SUB = 512
HALO = 16
VMEM_LIMIT = 56 * 1024 * 1024


def _params(n_axes, vmem=VMEM_LIMIT):
    return pltpu.CompilerParams(dimension_semantics=("arbitrary",) * n_axes, vmem_limit_bytes=vmem)


def _mod_row(tile, tm):
    return jnp.minimum(tile // (SEQ // tm), BATCH)


def _rms(x, g):
    return x * lax.rsqrt(jnp.mean(x * x, axis=-1, keepdims=True) + EPS) * g


def _ada_kernel(cond_ref, w_ref, b_ref, o_ref):
    cnd = cond_ref[...]
    s = (cnd * jax.nn.sigmoid(cnd)).astype(BF16)
    w = w_ref[...].astype(BF16)
    o_ref[...] = jnp.dot(s, w, preferred_element_type=F32) + b_ref[...]


def _adaln_all(cond, w_ada, b_ada):
    n = N_MOD * D_MODEL
    return pl.pallas_call(
        _ada_kernel,
        grid=(DEPTH, n // ADA_TN),
        in_specs=[
            pl.BlockSpec((MOD_ROWS, D_MODEL), lambda l, j: (0, 0)),
            pl.BlockSpec((None, D_MODEL, ADA_TN), lambda l, j: (l, 0, j)),
            pl.BlockSpec((None, 1, ADA_TN), lambda l, j: (l, 0, j)),
        ],
        out_specs=pl.BlockSpec((None, MOD_ROWS, ADA_TN), lambda l, j: (l, 0, j)),
        out_shape=jax.ShapeDtypeStruct((DEPTH, MOD_ROWS, n), F32),
        compiler_params=_params(2),
        name="adaln",
    )(cond, w_ada, b_ada.reshape(DEPTH, 1, n))


_KIND_Q, _KIND_KV, _KIND_GC, _KIND_U, _KIND_GB = range(5)
_QB = ATTN_WIDTH // INPROJ_TN
_CB = CONV_WIDTH // INPROJ_TN
_GB0 = (ATTN_WIDTH + 2 * KV_WIDTH) // INPROJ_TN
_STEPS = ([(_KIND_Q, b) for b in range(_QB)] + [(_KIND_KV, _QB)]
          + [kb for c in range(_CB) for kb in ((_KIND_GC, _GB0 + _CB + c), (_KIND_U, _GB0 + 2 * _CB + c))]
          + [(_KIND_GB, _GB0 + c) for c in range(_CB)])
_NB = len(_STEPS)
_NORM_ROWS = INPROJ_TM // (_NB - 1)


def _lookup(values, idx):
    out = jnp.int32(values[0])
    for n, v in enumerate(values[1:], start=1):
        out = jnp.where(idx == n, jnp.int32(v), out)
    return out


def _rope(x, cos, sin_a, sin_b):
    return (x * cos + pltpu.roll(x, HEAD_DIM - AXIS_DIM // 2, axis=1) * sin_a
            + pltpu.roll(x, AXIS_DIM // 2, axis=1) * sin_b)


def _inproj_kernel(x_ref, sh_ref, sc_ref, n1g_ref, w_ref, qg_ref, kg_ref, cos_ref, sa_ref, sb_ref,
                   q_ref, kv_ref, z_ref, gb_ref, h_cur, h_next, r0_scr, r1_scr, r2_scr):
    t = pl.program_id(0)
    tm = x_ref.shape[0]
    r_scr = (r0_scr, r1_scr, r2_scr)
    assert _NB % len(r_scr) == 0

    def norm_rows(r0, dst):
        x = x_ref[r0:r0 + _NORM_ROWS, :]
        h = _rms(x, n1g_ref[...]) * (1.0 + sc_ref[...]) + sh_ref[...]
        dst[r0:r0 + _NORM_ROWS, :] = h.astype(BF16)

    def head(r_ref, col0, gain, scale):
        y = _rms(r_ref[:, col0:col0 + HEAD_DIM], gain)
        y = _rope(y, cos_ref[...], sa_ref[...], sb_ref[...])
        return (y * scale).astype(BF16)

    def finish(kind, r_ref, r_before):
        if kind == _KIND_Q:
            for hd in range(INPROJ_TN // HEAD_DIM):
                q_ref[:, hd * HEAD_DIM:(hd + 1) * HEAD_DIM] = head(r_ref, hd * HEAD_DIM, qg_ref[...],
                                                                   HEAD_DIM ** -0.5 * LOG2_E)
        elif kind == _KIND_KV:
            for hd in range(N_KV_HEADS):
                kv_ref[:, hd * HEAD_DIM:(hd + 1) * HEAD_DIM] = head(r_ref, hd * HEAD_DIM, kg_ref[...], 1.0)
                v0 = KV_WIDTH + hd * 2 * HEAD_DIM
                kv_ref[:, v0:v0 + HEAD_DIM] = r_ref[:, KV_WIDTH + hd * HEAD_DIM:
                                                    KV_WIDTH + (hd + 1) * HEAD_DIM].astype(BF16)
                kv_ref[:, v0 + HEAD_DIM:v0 + 2 * HEAD_DIM] = jnp.ones((tm, HEAD_DIM), BF16)
        elif kind == _KIND_U:
            z_ref[...] = (r_before[...] * r_ref[...]).astype(BF16)
        elif kind == _KIND_GB:
            gb_ref[...] = r_ref[...].astype(BF16)

    @pl.when(t == 0)
    def _():
        for c in range(tm // _NORM_ROWS):
            norm_rows(c * _NORM_ROWS, h_cur)
        r_scr[0][...] = jnp.dot(h_cur[...], w_ref[...], preferred_element_type=F32)

    for jp in range(_NB):
        @pl.when(jnp.logical_and(t > 0, (t + _NB - 1) % _NB == jp))
        def _(jp=jp):
            r_scr[(jp + 1) % 3][...] = jnp.dot(h_cur[...], w_ref[...], preferred_element_type=F32)
            if jp < _NB - 1:
                norm_rows(jp * _NORM_ROWS, h_next)
            finish(_STEPS[jp][0], r_scr[jp % 3], r_scr[(jp - 1) % 3])
            if jp == _NB - 2:
                h_cur[...] = h_next[...]


def _in_proj(xs, mods, layer, n1g, w_in, qg, kg, cos_t, sa_t, sb_t):
    tm, tn = INPROJ_TM, INPROJ_TN
    n_tiles = N_TOKENS // tm
    lat_tiles = SEQ // tm
    assert tm % (_NB - 1) == 0 and _NORM_ROWS % 16 == 0

    def norm_tile(t):
        return jnp.where(t == 0, 0, jnp.minimum(t // _NB + 1, n_tiles - 1))

    def done(t):
        tp = jnp.maximum(t - 1, 0)
        return tp // _NB, tp % _NB

    def mod_spec(chunk):
        return pl.BlockSpec((None, None, 1, D_MODEL),
                            lambda t: (layer, _mod_row(norm_tile(t), tm), 0, chunk))

    def tab_spec():
        def index(t):
            i, _ = done(t)
            return (jnp.where(i < N_LATENT // tm, i % lat_tiles, lat_tiles), 0)
        return pl.BlockSpec((tm, HEAD_DIM), index)

    def after(kind):
        firsts = [n for n, (k, _) in enumerate(_STEPS) if k == kind]

        def index(t):
            i, j = done(t)
            col = jnp.int32(0)
            for n in firsts[1:]:
                col = col + (j >= n).astype(jnp.int32)
            return (i, col)
        return index

    return pl.pallas_call(
        _inproj_kernel,
        grid=(n_tiles * _NB + 1,),
        in_specs=[
            pl.BlockSpec((tm, D_MODEL), lambda t: (norm_tile(t), 0)),
            mod_spec(0), mod_spec(1),
            pl.BlockSpec((1, D_MODEL), lambda t: (0, 0)),
            pl.BlockSpec((D_MODEL, tn), lambda t: (0, _lookup([b for _, b in _STEPS], t % _NB))),
            pl.BlockSpec((1, HEAD_DIM), lambda t: (0, 0)),
            pl.BlockSpec((1, HEAD_DIM), lambda t: (0, 0)),
            tab_spec(), tab_spec(), tab_spec(),
        ],
        out_specs=[
            pl.BlockSpec((tm, tn), after(_KIND_Q)),
            pl.BlockSpec((tm, KV_COLS), lambda t: (done(t)[0], 0)),
            pl.BlockSpec((tm, tn), after(_KIND_U)),
            pl.BlockSpec((tm, tn), after(_KIND_GB)),
        ],
        out_shape=[
            jax.ShapeDtypeStruct((N_TOKENS, ATTN_WIDTH), BF16),
            jax.ShapeDtypeStruct((N_TOKENS, KV_COLS), BF16),
            jax.ShapeDtypeStruct((N_TOKENS, CONV_WIDTH), BF16),
            jax.ShapeDtypeStruct((N_TOKENS, CONV_WIDTH), BF16),
        ],
        scratch_shapes=[pltpu.VMEM((tm, D_MODEL), BF16)] * 2 + [pltpu.VMEM((tm, tn), F32)] * 3,
        compiler_params=_params(1),
        name="in_proj",
    )(xs, mods, mods, n1g, w_in, qg, kg, cos_t, sa_t, sb_t)


_NQ = SEQ // ATTN_TQ


def _attend(q, chunks):
    def scores(chunk):
        k_ref, _, start, size = chunk
        return lax.dot_general(q, k_ref[start:start + size, :], (((1,), (1,)), ((), ())),
                               preferred_element_type=F32)

    m = acc = None
    s = scores(chunks[0])
    for idx, (_, v_ref, start, size) in enumerate(chunks):
        s_next = scores(chunks[idx + 1]) if idx + 1 < len(chunks) else None
        m_c = jnp.max(s, axis=-1, keepdims=True)
        m_new = m_c if m is None else jnp.maximum(m, m_c)
        p = jnp.exp2(s - m_new).astype(BF16)
        pv = jnp.dot(p, v_ref[start:start + size, :], preferred_element_type=F32)
        acc = pv if m is None else jnp.exp2(m - m_new) * acc + pv
        m, s = m_new, s_next
    return acc[:, :HEAD_DIM] / acc[:, HEAD_DIM:]


def _attn_kernel(q_ref, kl_ref, vl_ref, kc_ref, vc_ref, o_ref):
    qi = pl.program_id(2)
    tq = q_ref.shape[0]
    ctx_chunk = [(kc_ref, vc_ref, 0, CTX_LEN)]
    lat_chunks = [(kl_ref, vl_ref, c * ATTN_TK, ATTN_TK) for c in range(SEQ // ATTN_TK)]

    def run(chunks):
        q = jnp.concatenate([q_ref[:, r * HEAD_DIM:(r + 1) * HEAD_DIM] for r in range(Q_PER_KV)], axis=0)
        o = _attend(q, chunks).astype(BF16)
        for r in range(Q_PER_KV):
            o_ref[:, r * HEAD_DIM:(r + 1) * HEAD_DIM] = o[r * tq:(r + 1) * tq, :]

    @pl.when(qi < _NQ)
    def _():
        run(lat_chunks + ctx_chunk)

    @pl.when(qi == _NQ)
    def _():
        run(ctx_chunk)


def _attention(q, kv):
    tq = ATTN_TQ
    ctx_blk0 = N_LATENT // CTX_LEN
    v_blk0 = KV_WIDTH // (2 * HEAD_DIM)

    def q_map(b, g, qi):
        return (jnp.where(qi < _NQ, b * _NQ + qi, N_LATENT // tq + b), g)

    return pl.pallas_call(
        _attn_kernel,
        grid=(BATCH, N_KV_HEADS, _NQ + 1),
        in_specs=[
            pl.BlockSpec((tq, Q_PER_KV * HEAD_DIM), q_map),
            pl.BlockSpec((SEQ, HEAD_DIM), lambda b, g, qi: (b, g)),
            pl.BlockSpec((SEQ, 2 * HEAD_DIM), lambda b, g, qi: (b, v_blk0 + g)),
            pl.BlockSpec((CTX_LEN, HEAD_DIM), lambda b, g, qi: (ctx_blk0 + b, g)),
            pl.BlockSpec((CTX_LEN, 2 * HEAD_DIM), lambda b, g, qi: (ctx_blk0 + b, v_blk0 + g)),
        ],
        out_specs=pl.BlockSpec((tq, Q_PER_KV * HEAD_DIM), q_map),
        out_shape=jax.ShapeDtypeStruct((N_TOKENS, ATTN_WIDTH), BF16),
        compiler_params=_params(3),
        name="attention",
    )(q, kv, kv, kv, kv)


def _merge_kernel(x_ref, attn_ref, z_ref, zp_ref, zn_ref, gb_ref, cw_ref, cb_ref, ag_ref, cg_ref, w_ref,
                  g1_ref, sh2_ref, sc2_ref, n2g_ref, xo_ref, h2_ref, cat_scr):
    tm = x_ref.shape[0]
    i = pl.program_id(0)

    row = lax.broadcasted_iota(jnp.int32, (tm, 1), 0)
    grow = i * tm + row
    seg = jnp.where(grow >= N_LATENT, CTX_LEN, SEQ)
    pos = jnp.bitwise_and(grow, seg - 1)

    z = z_ref[...].astype(F32)
    z_prev = jnp.where(row == 0, zp_ref[HALO - 1:HALO, :].astype(F32), pltpu.roll(z, 1, axis=0))
    z_prev = jnp.where(pos != 0, z_prev, 0.0)
    z_next = jnp.where(row == tm - 1, zn_ref[0:1, :].astype(F32), pltpu.roll(z, tm - 1, axis=0))
    z_next = jnp.where(pos != seg - 1, z_next, 0.0)
    conv = z_prev * cw_ref[0:1, :] + z * cw_ref[1:2, :] + z_next * cw_ref[2:3, :] + cb_ref[...]
    conv = gb_ref[...].astype(F32) * conv

    cat_scr[:, :ATTN_WIDTH] = _rms(attn_ref[...].astype(F32), ag_ref[...]).astype(BF16)
    cat_scr[:, ATTN_WIDTH:] = _rms(conv, cg_ref[...]).astype(BF16)
    y = jnp.dot(cat_scr[...], w_ref[...], preferred_element_type=F32)
    x_new = x_ref[...] + g1_ref[...] * y
    xo_ref[...] = x_new
    h2 = _rms(x_new, n2g_ref[...]) * (1.0 + sc2_ref[...]) + sh2_ref[...]
    h2_ref[...] = h2.astype(BF16)


def _merge(xs, attn, z, gb, mods, layer, cw, cb, ag, cg, w_out, n2g, n_rows):
    tm = MERGE_TM
    hb = tm // HALO
    last_hb = N_TOKENS // HALO - 1

    def mod_spec(chunk):
        return pl.BlockSpec((None, None, 1, D_MODEL), lambda i: (layer, _mod_row(i, tm), 0, chunk))

    def vec_spec(rows, width):
        return pl.BlockSpec((rows, width), lambda i: (0, 0))

    return pl.pallas_call(
        _merge_kernel,
        grid=(n_rows // tm,),
        in_specs=[
            pl.BlockSpec((tm, D_MODEL), lambda i: (i, 0)),
            pl.BlockSpec((tm, ATTN_WIDTH), lambda i: (i, 0)),
            pl.BlockSpec((tm, CONV_WIDTH), lambda i: (i, 0)),
            pl.BlockSpec((HALO, CONV_WIDTH), lambda i: (jnp.maximum(i * hb - 1, 0), 0)),
            pl.BlockSpec((HALO, CONV_WIDTH), lambda i: (jnp.minimum((i + 1) * hb, last_hb), 0)),
            pl.BlockSpec((tm, CONV_WIDTH), lambda i: (i, 0)),
            vec_spec(3, CONV_WIDTH), vec_spec(1, CONV_WIDTH),
            vec_spec(1, ATTN_WIDTH), vec_spec(1, CONV_WIDTH),
            pl.BlockSpec((D_MODEL, D_MODEL), lambda i: (0, 0)),
            mod_spec(2), mod_spec(3), mod_spec(4),
            vec_spec(1, D_MODEL),
        ],
        out_specs=[
            pl.BlockSpec((tm, D_MODEL), lambda i: (i, 0)),
            pl.BlockSpec((tm, D_MODEL), lambda i: (i, 0)),
        ],
        out_shape=[
            jax.ShapeDtypeStruct((n_rows, D_MODEL), F32),
            jax.ShapeDtypeStruct((n_rows, D_MODEL), BF16),
        ],
        scratch_shapes=[pltpu.VMEM((tm, D_MODEL), BF16)],
        compiler_params=_params(1),
        name="merge_out_proj",
    )(xs, attn, z, z, z, gb, cw, cb, ag, cg, w_out, mods, mods, mods, n2g)


def _mlp_kernel(h_ref, w1_ref, w2_ref, x_ref, g2_ref, o_ref, acc_scr):
    k = pl.program_id(1)

    @pl.when(k == 0)
    def _():
        acc_scr[...] = jnp.zeros_like(acc_scr)

    def up(s):
        return jnp.dot(h_ref[...], w1_ref[:, s * MLP_SUB:(s + 1) * MLP_SUB], preferred_element_type=F32)

    n_sub = w1_ref.shape[1] // MLP_SUB
    part = None
    a_next = up(0)
    for s in range(n_sub):
        a = a_next
        if s + 1 < n_sub:
            a_next = up(s + 1)
        a = jnp.maximum(a, 0.0)
        d = jnp.dot((a * a).astype(BF16), w2_ref[s * MLP_SUB:(s + 1) * MLP_SUB, :], preferred_element_type=F32)
        part = d if part is None else part + d
    acc_scr[...] += part

    @pl.when(k == pl.num_programs(1) - 1)
    def _():
        o_ref[...] = x_ref[...] + g2_ref[...] * acc_scr[...]


def _mlp(h2, x_new, mods, layer, w1, w2, n_rows):
    tm, th = MLP_TM, MLP_TH
    return pl.pallas_call(
        _mlp_kernel,
        grid=(n_rows // tm, MLP_HIDDEN // th),
        in_specs=[
            pl.BlockSpec((tm, D_MODEL), lambda i, k: (i, 0)),
            pl.BlockSpec((D_MODEL, th), lambda i, k: (0, k)),
            pl.BlockSpec((th, D_MODEL), lambda i, k: (k, 0)),
            pl.BlockSpec((tm, D_MODEL), lambda i, k: (i, 0)),
            pl.BlockSpec((None, None, 1, D_MODEL), lambda i, k: (layer, _mod_row(i, tm), 0, 5)),
        ],
        out_specs=pl.BlockSpec((tm, D_MODEL), lambda i, k: (i, 0)),
        out_shape=jax.ShapeDtypeStruct((n_rows, D_MODEL), F32),
        scratch_shapes=[pltpu.VMEM((tm, D_MODEL), F32)],
        compiler_params=_params(2),
        name="mlp",
    )(h2, w1, w2, x_new, mods)


def _rope_tables(tm):
    t = jnp.arange(SEQ)
    row = (t // GRID_W).astype(F32)
    col = (t % GRID_W).astype(F32)
    inv_freq = ROPE_THETA ** (-jnp.arange(0, AXIS_DIM, 2, dtype=F32) / AXIS_DIM)
    ang_r = row[:, None] * inv_freq[None, :]
    ang_c = col[:, None] * inv_freq[None, :]
    ang = jnp.concatenate([ang_r, ang_r, ang_c, ang_c], axis=-1)
    cos, sin = jnp.cos(ang), jnp.sin(ang)
    first = (jnp.arange(HEAD_DIM) % AXIS_DIM) < AXIS_DIM // 2
    sin_a = jnp.where(first, -sin, 0.0)
    sin_b = jnp.where(first, 0.0, sin)
    ident = jnp.zeros((tm, HEAD_DIM), F32)
    return (jnp.concatenate([cos, ident + 1.0]), jnp.concatenate([sin_a, ident]),
            jnp.concatenate([sin_b, ident]))


def kernel(x, c, ctx, c_ctx, w_ada, b_ada, norm1_g, w_in, q_norm_g, k_norm_g, conv_w, conv_b,
           attn_out_g, conv_out_g, w_out, norm2_g, w_mlp_in, w_mlp_out):
    assert x.shape == (BATCH, SEQ, D_MODEL) and ctx.shape == (BATCH, CTX_LEN, D_MODEL)
    xs = jnp.concatenate([x.reshape(N_LATENT, D_MODEL), ctx.reshape(N_CTX, D_MODEL)], axis=0)
    cond = jnp.concatenate([c, c_ctx[None, :], jnp.zeros((MOD_ROWS - BATCH - 1, D_MODEL), F32)], axis=0)
    mods = _adaln_all(cond, w_ada, b_ada).reshape(DEPTH, MOD_ROWS, 1, N_MOD * D_MODEL)
    cos_t, sa_t, sb_t = _rope_tables(INPROJ_TM)

    for l in range(DEPTH):
        q, kv, z, gb = _in_proj(xs, mods, l, norm1_g[l][None], w_in[l].astype(BF16),
                                q_norm_g[l][None], k_norm_g[l][None], cos_t, sa_t, sb_t)
        attn = _attention(q, kv)
        n_rows = N_TOKENS if l < DEPTH - 1 else N_LATENT
        x_new, h2 = _merge(xs, attn, z, gb, mods, l, conv_w[l], conv_b[l][None], attn_out_g[l][None],
                           conv_out_g[l][None], w_out[l].astype(BF16), norm2_g[l][None], n_rows)
        xs = _mlp(h2, x_new, mods, l, w_mlp_in[l].astype(BF16), w_mlp_out[l].astype(BF16), n_rows)
    return xs.reshape(BATCH, SEQ, D_MODEL)
```

```python
import jax
import jax.numpy as jnp
from jax import lax
from jax.experimental import pallas as pl
from jax.experimental.pallas import tpu as pltpu

D_MODEL = 2048
BATCH = 4
SEQ = 4096
DEPTH = 4
CTX_LEN = 256
GRID_W = 64
HEAD_DIM = 128
N_HEADS = 8
N_KV_HEADS = 2
Q_PER_KV = N_HEADS // N_KV_HEADS
ATTN_WIDTH = N_HEADS * HEAD_DIM
KV_WIDTH = N_KV_HEADS * HEAD_DIM
CONV_WIDTH = D_MODEL - ATTN_WIDTH
IN_WIDTH = ATTN_WIDTH + 2 * KV_WIDTH + 3 * CONV_WIDTH
MLP_HIDDEN = 4 * D_MODEL
N_MOD = 6
ROPE_THETA = 10000.0
AXIS_DIM = HEAD_DIM // 2
EPS = 1e-6
LOG2_E = 1.4426950408889634
KV_COLS = KV_WIDTH + 2 * KV_WIDTH

N_LATENT = BATCH * SEQ
N_CTX = BATCH * CTX_LEN
N_TOKENS = N_LATENT + N_CTX
MOD_ROWS = 8

F32 = jnp.float32
BF16 = jnp.bfloat16

ADA_TN = 1024
INPROJ_TM = 1024
INPROJ_TN = 512
ATTN_TQ = 256
ATTN_TK = 512
MERGE_TM = 256
MLP_TM = 512
MLP_TH = 1024
MLP_SUB = 512
HALO = 16
VMEM_LIMIT = 56 * 1024 * 1024


def _params(n_axes, vmem=VMEM_LIMIT):
    return pltpu.CompilerParams(dimension_semantics=("arbitrary",) * n_axes, vmem_limit_bytes=vmem)


def _mod_row(tile, tm):
    return jnp.minimum(tile // (SEQ // tm), BATCH)


def _rms(x, g):
    return x * lax.rsqrt(jnp.mean(x * x, axis=-1, keepdims=True) + EPS) * g


def _ada_kernel(cond_ref, w_ref, b_ref, o_ref):
    cnd = cond_ref[...]
    s = (cnd * jax.nn.sigmoid(cnd)).astype(BF16)
    w = w_ref[...].astype(BF16)
    o_ref[...] = jnp.dot(s, w, preferred_element_type=F32) + b_ref[...]


def _adaln_all(cond, w_ada, b_ada):
    n = N_MOD * D_MODEL
    return pl.pallas_call(
        _ada_kernel,
        grid=(DEPTH, n // ADA_TN),
        in_specs=[
            pl.BlockSpec((MOD_ROWS, D_MODEL), lambda l, j: (0, 0)),
            pl.BlockSpec((None, D_MODEL, ADA_TN), lambda l, j: (l, 0, j)),
            pl.BlockSpec((None, 1, ADA_TN), lambda l, j: (l, 0, j)),
        ],
        out_specs=pl.BlockSpec((None, MOD_ROWS, ADA_TN), lambda l, j: (l, 0, j)),
        out_shape=jax.ShapeDtypeStruct((DEPTH, MOD_ROWS, n), F32),
        compiler_params=_params(2),
        name="adaln",
    )(cond, w_ada, b_ada.reshape(DEPTH, 1, n))


_KIND_Q, _KIND_KV, _KIND_GC, _KIND_U, _KIND_GB = range(5)
_QB = ATTN_WIDTH // INPROJ_TN
_CB = CONV_WIDTH // INPROJ_TN
_GB0 = (ATTN_WIDTH + 2 * KV_WIDTH) // INPROJ_TN
_STEPS = ([(_KIND_Q, b) for b in range(_QB)] + [(_KIND_KV, _QB)]
          + [kb for c in range(_CB) for kb in ((_KIND_GC, _GB0 + _CB + c), (_KIND_U, _GB0 + 2 * _CB + c))]
          + [(_KIND_GB, _GB0 + c) for c in range(_CB)])
_NB = len(_STEPS)
_NORM_ROWS = INPROJ_TM // (_NB - 1)


def _lookup(values, idx):
    out = jnp.int32(values[0])
    for n, v in enumerate(values[1:], start=1):
        out = jnp.where(idx == n, jnp.int32(v), out)
    return out


def _rope(x, cos, sin_a, sin_b):
    return (x * cos + pltpu.roll(x, HEAD_DIM - AXIS_DIM // 2, axis=1) * sin_a
            + pltpu.roll(x, AXIS_DIM // 2, axis=1) * sin_b)


def _inproj_kernel(x_ref, sh_ref, sc_ref, n1g_ref, w_ref, qg_ref, kg_ref, cos_ref, sa_ref, sb_ref,
                   q_ref, kv_ref, z_ref, gb_ref, h_cur, h_next, r0_scr, r1_scr, r2_scr):
    t = pl.program_id(0)
    tm = x_ref.shape[0]
    r_scr = (r0_scr, r1_scr, r2_scr)
    assert _NB % len(r_scr) == 0

    def norm_rows(r0, dst):
        x = x_ref[r0:r0 + _NORM_ROWS, :]
        h = _rms(x, n1g_ref[...]) * (1.0 + sc_ref[...]) + sh_ref[...]
        dst[r0:r0 + _NORM_ROWS, :] = h.astype(BF16)

    def head(r_ref, col0, gain, scale):
        y = _rms(r_ref[:, col0:col0 + HEAD_DIM], gain)
        y = _rope(y, cos_ref[...], sa_ref[...], sb_ref[...])
        return (y * scale).astype(BF16)

    def finish(kind, r_ref, r_before):
        if kind == _KIND_Q:
            for hd in range(INPROJ_TN // HEAD_DIM):
                q_ref[:, hd * HEAD_DIM:(hd + 1) * HEAD_DIM] = head(r_ref, hd * HEAD_DIM, qg_ref[...],
                                                                   HEAD_DIM ** -0.5 * LOG2_E)
        elif kind == _KIND_KV:
            for hd in range(N_KV_HEADS):
                kv_ref[:, hd * HEAD_DIM:(hd + 1) * HEAD_DIM] = head(r_ref, hd * HEAD_DIM, kg_ref[...], 1.0)
                v0 = KV_WIDTH + hd * 2 * HEAD_DIM
                kv_ref[:, v0:v0 + HEAD_DIM] = r_ref[:, KV_WIDTH + hd * HEAD_DIM:
                                                    KV_WIDTH + (hd + 1) * HEAD_DIM].astype(BF16)
                kv_ref[:, v0 + HEAD_DIM:v0 + 2 * HEAD_DIM] = jnp.ones((tm, HEAD_DIM), BF16)
        elif kind == _KIND_U:
            z_ref[...] = (r_before[...] * r_ref[...]).astype(BF16)
        elif kind == _KIND_GB:
            gb_ref[...] = r_ref[...].astype(BF16)

    @pl.when(t == 0)
    def _():
        for c in range(tm // _NORM_ROWS):
            norm_rows(c * _NORM_ROWS, h_cur)
        r_scr[0][...] = jnp.dot(h_cur[...], w_ref[...], preferred_element_type=F32)

    for jp in range(_NB):
        @pl.when(jnp.logical_and(t > 0, (t + _NB - 1) % _NB == jp))
        def _(jp=jp):
            r_scr[(jp + 1) % 3][...] = jnp.dot(h_cur[...], w_ref[...], preferred_element_type=F32)
            if jp < _NB - 1:
                norm_rows(jp * _NORM_ROWS, h_next)
            finish(_STEPS[jp][0], r_scr[jp % 3], r_scr[(jp - 1) % 3])
            if jp == _NB - 2:
                h_cur[...] = h_next[...]


def _in_proj(xs, mods, layer, n1g, w_in, qg, kg, cos_t, sa_t, sb_t):
    tm, tn = INPROJ_TM, INPROJ_TN
    n_tiles = N_TOKENS // tm
    lat_tiles = SEQ // tm
    assert tm % (_NB - 1) == 0 and _NORM_ROWS % 16 == 0

    def norm_tile(t):
        return jnp.where(t == 0, 0, jnp.minimum(t // _NB + 1, n_tiles - 1))

    def done(t):
        tp = jnp.maximum(t - 1, 0)
        return tp // _NB, tp % _NB

    def mod_spec(chunk):
        return pl.BlockSpec((None, None, 1, D_MODEL),
                            lambda t: (layer, _mod_row(norm_tile(t), tm), 0, chunk))

    def tab_spec():
        def index(t):
            i, _ = done(t)
            return (jnp.where(i < N_LATENT // tm, i % lat_tiles, lat_tiles), 0)
        return pl.BlockSpec((tm, HEAD_DIM), index)

    def after(kind):
        firsts = [n for n, (k, _) in enumerate(_STEPS) if k == kind]

        def index(t):
            i, j = done(t)
            col = jnp.int32(0)
            for n in firsts[1:]:
                col = col + (j >= n).astype(jnp.int32)
            return (i, col)
        return index

    return pl.pallas_call(
        _inproj_kernel,
        grid=(n_tiles * _NB + 1,),
        in_specs=[
            pl.BlockSpec((tm, D_MODEL), lambda t: (norm_tile(t), 0)),
            mod_spec(0), mod_spec(1),
            pl.BlockSpec((1, D_MODEL), lambda t: (0, 0)),
            pl.BlockSpec((None, D_MODEL, tn), lambda t: (_lookup([b for _, b in _STEPS], t % _NB), 0, 0)),
            pl.BlockSpec((1, HEAD_DIM), lambda t: (0, 0)),
            pl.BlockSpec((1, HEAD_DIM), lambda t: (0, 0)),
            tab_spec(), tab_spec(), tab_spec(),
        ],
        out_specs=[
            pl.BlockSpec((tm, tn), after(_KIND_Q)),
            pl.BlockSpec((tm, KV_COLS), lambda t: (done(t)[0], 0)),
            pl.BlockSpec((tm, tn), after(_KIND_U)),
            pl.BlockSpec((tm, tn), after(_KIND_GB)),
        ],
        out_shape=[
            jax.ShapeDtypeStruct((N_TOKENS, ATTN_WIDTH), BF16),
            jax.ShapeDtypeStruct((N_TOKENS, KV_COLS), BF16),
            jax.ShapeDtypeStruct((N_TOKENS, CONV_WIDTH), BF16),
            jax.ShapeDtypeStruct((N_TOKENS, CONV_WIDTH), BF16),
        ],
        scratch_shapes=[pltpu.VMEM((tm, D_MODEL), BF16)] * 2 + [pltpu.VMEM((tm, tn), F32)] * 3,
        compiler_params=_params(1),
        name="in_proj",
    )(xs, mods, mods, n1g, w_in, qg, kg, cos_t, sa_t, sb_t)


_NQ = SEQ // ATTN_TQ


def _attend(q, chunks):
    def scores(chunk):
        k_ref, _, start, size = chunk
        return lax.dot_general(q, k_ref[start:start + size, :], (((1,), (1,)), ((), ())),
                               preferred_element_type=F32)

    m = acc = None
    s = scores(chunks[0])
    for idx, (_, v_ref, start, size) in enumerate(chunks):
        s_next = scores(chunks[idx + 1]) if idx + 1 < len(chunks) else None
        m_c = jnp.max(s, axis=-1, keepdims=True)
        m_new = m_c if m is None else jnp.maximum(m, m_c)
        p = jnp.exp2(s - m_new).astype(BF16)
        pv = jnp.dot(p, v_ref[start:start + size, :], preferred_element_type=F32)
        acc = pv if m is None else jnp.exp2(m - m_new) * acc + pv
        m, s = m_new, s_next
    return acc[:, :HEAD_DIM] / acc[:, HEAD_DIM:]


def _attn_kernel(q_ref, kl_ref, vl_ref, kc_ref, vc_ref, o_ref):
    qi = pl.program_id(2)
    tq = q_ref.shape[0]
    ctx_chunk = [(kc_ref, vc_ref, 0, CTX_LEN)]
    lat_chunks = [(kl_ref, vl_ref, c * ATTN_TK, ATTN_TK) for c in range(SEQ // ATTN_TK)]

    def run(chunks):
        q = jnp.concatenate([q_ref[:, r * HEAD_DIM:(r + 1) * HEAD_DIM] for r in range(Q_PER_KV)], axis=0)
        o = _attend(q, chunks).astype(BF16)
        for r in range(Q_PER_KV):
            o_ref[:, r * HEAD_DIM:(r + 1) * HEAD_DIM] = o[r * tq:(r + 1) * tq, :]

    @pl.when(qi < _NQ)
    def _():
        run(lat_chunks + ctx_chunk)

    @pl.when(qi == _NQ)
    def _():
        run(ctx_chunk)


def _attention(q, kv):
    tq = ATTN_TQ
    ctx_blk0 = N_LATENT // CTX_LEN
    v_blk0 = KV_WIDTH // (2 * HEAD_DIM)

    def q_map(b, g, qi):
        return (jnp.where(qi < _NQ, b * _NQ + qi, N_LATENT // tq + b), g)

    return pl.pallas_call(
        _attn_kernel,
        grid=(BATCH, N_KV_HEADS, _NQ + 1),
        in_specs=[
            pl.BlockSpec((tq, Q_PER_KV * HEAD_DIM), q_map),
            pl.BlockSpec((SEQ, HEAD_DIM), lambda b, g, qi: (b, g)),
            pl.BlockSpec((SEQ, 2 * HEAD_DIM), lambda b, g, qi: (b, v_blk0 + g)),
            pl.BlockSpec((CTX_LEN, HEAD_DIM), lambda b, g, qi: (ctx_blk0 + b, g)),
            pl.BlockSpec((CTX_LEN, 2 * HEAD_DIM), lambda b, g, qi: (ctx_blk0 + b, v_blk0 + g)),
        ],
        out_specs=pl.BlockSpec((tq, Q_PER_KV * HEAD_DIM), q_map),
        out_shape=jax.ShapeDtypeStruct((N_TOKENS, ATTN_WIDTH), BF16),
        compiler_params=_params(3),
        name="attention",
    )(q, kv, kv, kv, kv)


def _merge_kernel(x_ref, attn_ref, z_ref, zp_ref, zn_ref, gb_ref, cw_ref, cb_ref, ag_ref, cg_ref, w_ref,
                  g1_ref, sh2_ref, sc2_ref, n2g_ref, xo_ref, h2_ref, cat_scr):
    tm = x_ref.shape[0]
    i = pl.program_id(0)

    row = lax.broadcasted_iota(jnp.int32, (tm, 1), 0)
    grow = i * tm + row
    seg = jnp.where(grow >= N_LATENT, CTX_LEN, SEQ)
    pos = jnp.bitwise_and(grow, seg - 1)

    z = z_ref[...].astype(F32)
    z_prev = jnp.where(row == 0, zp_ref[HALO - 1:HALO, :].astype(F32), pltpu.roll(z, 1, axis=0))
    z_prev = jnp.where(pos != 0, z_prev, 0.0)
    z_next = jnp.where(row == tm - 1, zn_ref[0:1, :].astype(F32), pltpu.roll(z, tm - 1, axis=0))
    z_next = jnp.where(pos != seg - 1, z_next, 0.0)
    conv = z_prev * cw_ref[0:1, :] + z * cw_ref[1:2, :] + z_next * cw_ref[2:3, :] + cb_ref[...]
    conv = gb_ref[...].astype(F32) * conv

    cat_scr[:, :ATTN_WIDTH] = _rms(attn_ref[...].astype(F32), ag_ref[...]).astype(BF16)
    cat_scr[:, ATTN_WIDTH:] = _rms(conv, cg_ref[...]).astype(BF16)
    y = jnp.dot(cat_scr[...], w_ref[...], preferred_element_type=F32)
    x_new = x_ref[...] + g1_ref[...] * y
    xo_ref[...] = x_new
    h2 = _rms(x_new, n2g_ref[...]) * (1.0 + sc2_ref[...]) + sh2_ref[...]
    h2_ref[...] = h2.astype(BF16)


def _merge(xs, attn, z, gb, mods, layer, cw, cb, ag, cg, w_out, n2g, n_rows):
    tm = MERGE_TM
    hb = tm // HALO
    last_hb = N_TOKENS // HALO - 1

    def mod_spec(chunk):
        return pl.BlockSpec((None, None, 1, D_MODEL), lambda i: (layer, _mod_row(i, tm), 0, chunk))

    def vec_spec(rows, width):
        return pl.BlockSpec((rows, width), lambda i: (0, 0))

    return pl.pallas_call(
        _merge_kernel,
        grid=(n_rows // tm,),
        in_specs=[
            pl.BlockSpec((tm, D_MODEL), lambda i: (i, 0)),
            pl.BlockSpec((tm, ATTN_WIDTH), lambda i: (i, 0)),
            pl.BlockSpec((tm, CONV_WIDTH), lambda i: (i, 0)),
            pl.BlockSpec((HALO, CONV_WIDTH), lambda i: (jnp.maximum(i * hb - 1, 0), 0)),
            pl.BlockSpec((HALO, CONV_WIDTH), lambda i: (jnp.minimum((i + 1) * hb, last_hb), 0)),
            pl.BlockSpec((tm, CONV_WIDTH), lambda i: (i, 0)),
            vec_spec(3, CONV_WIDTH), vec_spec(1, CONV_WIDTH),
            vec_spec(1, ATTN_WIDTH), vec_spec(1, CONV_WIDTH),
            pl.BlockSpec((D_MODEL, D_MODEL), lambda i: (0, 0)),
            mod_spec(2), mod_spec(3), mod_spec(4),
            vec_spec(1, D_MODEL),
        ],
        out_specs=[
            pl.BlockSpec((tm, D_MODEL), lambda i: (i, 0)),
            pl.BlockSpec((tm, D_MODEL), lambda i: (i, 0)),
        ],
        out_shape=[
            jax.ShapeDtypeStruct((n_rows, D_MODEL), F32),
            jax.ShapeDtypeStruct((n_rows, D_MODEL), BF16),
        ],
        scratch_shapes=[pltpu.VMEM((tm, D_MODEL), BF16)],
        compiler_params=_params(1),
        name="merge_out_proj",
    )(xs, attn, z, z, z, gb, cw, cb, ag, cg, w_out, mods, mods, mods, n2g)


def _mlp_kernel(h_ref, w1_ref, w2_ref, x_ref, g2_ref, o_ref, acc_scr):
    k = pl.program_id(1)

    @pl.when(k == 0)
    def _():
        acc_scr[...] = jnp.zeros_like(acc_scr)

    def up(s):
        return jnp.dot(h_ref[...], w1_ref[:, s * MLP_SUB:(s + 1) * MLP_SUB], preferred_element_type=F32)

    n_sub = w1_ref.shape[1] // MLP_SUB
    part = None
    a_next = up(0)
    for s in range(n_sub):
        a = a_next
        if s + 1 < n_sub:
            a_next = up(s + 1)
        a = jnp.maximum(a, 0.0)
        d = jnp.dot((a * a).astype(BF16), w2_ref[s * MLP_SUB:(s + 1) * MLP_SUB, :], preferred_element_type=F32)
        part = d if part is None else part + d
    acc_scr[...] += part

    @pl.when(k == pl.num_programs(1) - 1)
    def _():
        o_ref[...] = x_ref[...] + g2_ref[...] * acc_scr[...]


def _mlp(h2, x_new, mods, layer, w1, w2, n_rows):
    tm, th = MLP_TM, MLP_TH
    return pl.pallas_call(
        _mlp_kernel,
        grid=(n_rows // tm, MLP_HIDDEN // th),
        in_specs=[
            pl.BlockSpec((tm, D_MODEL), lambda i, k: (i, 0)),
            pl.BlockSpec((None, D_MODEL, th), lambda i, k: (k, 0, 0)),
            pl.BlockSpec((th, D_MODEL), lambda i, k: (k, 0)),
            pl.BlockSpec((tm, D_MODEL), lambda i, k: (i, 0)),
            pl.BlockSpec((None, None, 1, D_MODEL), lambda i, k: (layer, _mod_row(i, tm), 0, 5)),
        ],
        out_specs=pl.BlockSpec((tm, D_MODEL), lambda i, k: (i, 0)),
        out_shape=jax.ShapeDtypeStruct((n_rows, D_MODEL), F32),
        scratch_shapes=[pltpu.VMEM((tm, D_MODEL), F32)],
        compiler_params=_params(2),
        name="mlp",
    )(h2, w1, w2, x_new, mods)


def _rope_tables(tm):
    t = jnp.arange(SEQ)
    row = (t // GRID_W).astype(F32)
    col = (t % GRID_W).astype(F32)
    inv_freq = ROPE_THETA ** (-jnp.arange(0, AXIS_DIM, 2, dtype=F32) / AXIS_DIM)
    ang_r = row[:, None] * inv_freq[None, :]
    ang_c = col[:, None] * inv_freq[None, :]
    ang = jnp.concatenate([ang_r, ang_r, ang_c, ang_c], axis=-1)
    cos, sin = jnp.cos(ang), jnp.sin(ang)
    first = (jnp.arange(HEAD_DIM) % AXIS_DIM) < AXIS_DIM // 2
    sin_a = jnp.where(first, -sin, 0.0)
    sin_b = jnp.where(first, 0.0, sin)
    ident = jnp.zeros((tm, HEAD_DIM), F32)
    return (jnp.concatenate([cos, ident + 1.0]), jnp.concatenate([sin_a, ident]),
            jnp.concatenate([sin_b, ident]))


def _column_blocks(w, width):
    rows, cols = w.shape
    return w.astype(BF16).reshape(rows, cols // width, width).transpose(1, 0, 2)


def kernel(x, c, ctx, c_ctx, w_ada, b_ada, norm1_g, w_in, q_norm_g, k_norm_g, conv_w, conv_b,
           attn_out_g, conv_out_g, w_out, norm2_g, w_mlp_in, w_mlp_out):
    assert x.shape == (BATCH, SEQ, D_MODEL) and ctx.shape == (BATCH, CTX_LEN, D_MODEL)
    xs = jnp.concatenate([x.reshape(N_LATENT, D_MODEL), ctx.reshape(N_CTX, D_MODEL)], axis=0)
    cond = jnp.concatenate([c, c_ctx[None, :], jnp.zeros((MOD_ROWS - BATCH - 1, D_MODEL), F32)], axis=0)
    mods = _adaln_all(cond, w_ada, b_ada).reshape(DEPTH, MOD_ROWS, 1, N_MOD * D_MODEL)
    cos_t, sa_t, sb_t = _rope_tables(INPROJ_TM)

    for l in range(DEPTH):
        q, kv, z, gb = _in_proj(xs, mods, l, norm1_g[l][None], _column_blocks(w_in[l], INPROJ_TN),
                                q_norm_g[l][None], k_norm_g[l][None], cos_t, sa_t, sb_t)
        attn = _attention(q, kv)
        n_rows = N_TOKENS if l < DEPTH - 1 else N_LATENT
        x_new, h2 = _merge(xs, attn, z, gb, mods, l, conv_w[l], conv_b[l][None], attn_out_g[l][None],
                           conv_out_g[l][None], w_out[l].astype(BF16), norm2_g[l][None], n_rows)
        xs = _mlp(h2, x_new, mods, l, _column_blocks(w_mlp_in[l], MLP_TH), w_mlp_out[l].astype(BF16), n_rows)
    return xs.reshape(BATCH, SEQ, D_MODEL)
```

```python
import jax
import jax.numpy as jnp
from jax import lax
from jax.experimental import pallas as pl
from jax.experimental.pallas import tpu as pltpu

D_MODEL = 2048
BATCH = 4
SEQ = 4096
DEPTH = 4
CTX_LEN = 256
GRID_W = 64
HEAD_DIM = 128
N_HEADS = 8
N_KV_HEADS = 2
Q_PER_KV = N_HEADS // N_KV_HEADS
ATTN_WIDTH = N_HEADS * HEAD_DIM
KV_WIDTH = N_KV_HEADS * HEAD_DIM
CONV_WIDTH = D_MODEL - ATTN_WIDTH
IN_WIDTH = ATTN_WIDTH + 2 * KV_WIDTH + 3 * CONV_WIDTH
MLP_HIDDEN = 4 * D_MODEL
N_MOD = 6
ROPE_THETA = 10000.0
AXIS_DIM = HEAD_DIM // 2
EPS = 1e-6
LOG2_E = 1.4426950408889634
KV_COLS = KV_WIDTH + 2 * KV_WIDTH

N_LATENT = BATCH * SEQ
N_CTX = BATCH * CTX_LEN
N_TOKENS = N_LATENT + N_CTX
MOD_ROWS = 8

F32 = jnp.float32
BF16 = jnp.bfloat16

ADA_TN = 1024
INPROJ_TM = 1024
INPROJ_TN = 512
ATTN_TQ = 256
ATTN_TK = 512
MERGE_TM = 256
MLP_TM = 512
MLP_TH = 2048
MLP_SUB = 512
HALO = 16
VMEM_LIMIT = 56 * 1024 * 1024


def _params(n_axes, vmem=VMEM_LIMIT):
    return pltpu.CompilerParams(dimension_semantics=("arbitrary",) * n_axes, vmem_limit_bytes=vmem)


def _mod_row(tile, tm):
    return jnp.minimum(tile // (SEQ // tm), BATCH)


def _rms(x, g):
    return x * lax.rsqrt(jnp.mean(x * x, axis=-1, keepdims=True) + EPS) * g


def _ada_kernel(cond_ref, w_ref, b_ref, o_ref):
    cnd = cond_ref[...]
    s = (cnd * jax.nn.sigmoid(cnd)).astype(BF16)
    w = w_ref[...].astype(BF16)
    o_ref[...] = jnp.dot(s, w, preferred_element_type=F32) + b_ref[...]


def _adaln_all(cond, w_ada, b_ada):
    n = N_MOD * D_MODEL
    return pl.pallas_call(
        _ada_kernel,
        grid=(DEPTH, n // ADA_TN),
        in_specs=[
            pl.BlockSpec((MOD_ROWS, D_MODEL), lambda l, j: (0, 0)),
            pl.BlockSpec((None, D_MODEL, ADA_TN), lambda l, j: (l, 0, j)),
            pl.BlockSpec((None, 1, ADA_TN), lambda l, j: (l, 0, j)),
        ],
        out_specs=pl.BlockSpec((None, MOD_ROWS, ADA_TN), lambda l, j: (l, 0, j)),
        out_shape=jax.ShapeDtypeStruct((DEPTH, MOD_ROWS, n), F32),
        compiler_params=_params(2),
        name="adaln",
    )(cond, w_ada, b_ada.reshape(DEPTH, 1, n))


_KIND_Q, _KIND_KV, _KIND_GC, _KIND_U, _KIND_GB = range(5)
_QB = ATTN_WIDTH // INPROJ_TN
_CB = CONV_WIDTH // INPROJ_TN
_GB0 = (ATTN_WIDTH + 2 * KV_WIDTH) // INPROJ_TN
_STEPS = ([(_KIND_Q, b) for b in range(_QB)] + [(_KIND_KV, _QB)]
          + [kb for c in range(_CB) for kb in ((_KIND_GC, _GB0 + _CB + c), (_KIND_U, _GB0 + 2 * _CB + c))]
          + [(_KIND_GB, _GB0 + c) for c in range(_CB)])
_NB = len(_STEPS)
_NORM_ROWS = INPROJ_TM // (_NB - 1)


def _lookup(values, idx):
    out = jnp.int32(values[0])
    for n, v in enumerate(values[1:], start=1):
        out = jnp.where(idx == n, jnp.int32(v), out)
    return out


def _rope(x, cos, sin_a, sin_b):
    return (x * cos + pltpu.roll(x, HEAD_DIM - AXIS_DIM // 2, axis=1) * sin_a
            + pltpu.roll(x, AXIS_DIM // 2, axis=1) * sin_b)


def _inproj_kernel(x_ref, sh_ref, sc_ref, n1g_ref, w_ref, qg_ref, kg_ref, cos_ref, sa_ref, sb_ref,
                   q_ref, kv_ref, z_ref, gb_ref, h_cur, h_next, r0_scr, r1_scr, r2_scr):
    t = pl.program_id(0)
    tm = x_ref.shape[0]
    r_scr = (r0_scr, r1_scr, r2_scr)
    assert _NB % len(r_scr) == 0

    def norm_rows(r0, dst):
        x = x_ref[r0:r0 + _NORM_ROWS, :]
        h = _rms(x, n1g_ref[...]) * (1.0 + sc_ref[...]) + sh_ref[...]
        dst[r0:r0 + _NORM_ROWS, :] = h.astype(BF16)

    def head(r_ref, col0, gain, scale):
        y = _rms(r_ref[:, col0:col0 + HEAD_DIM], gain)
        y = _rope(y, cos_ref[...], sa_ref[...], sb_ref[...])
        return (y * scale).astype(BF16)

    def finish(kind, r_ref, r_before):
        if kind == _KIND_Q:
            for hd in range(INPROJ_TN // HEAD_DIM):
                q_ref[:, hd * HEAD_DIM:(hd + 1) * HEAD_DIM] = head(r_ref, hd * HEAD_DIM, qg_ref[...],
                                                                   HEAD_DIM ** -0.5 * LOG2_E)
        elif kind == _KIND_KV:
            for hd in range(N_KV_HEADS):
                kv_ref[:, hd * HEAD_DIM:(hd + 1) * HEAD_DIM] = head(r_ref, hd * HEAD_DIM, kg_ref[...], 1.0)
                v0 = KV_WIDTH + hd * 2 * HEAD_DIM
                kv_ref[:, v0:v0 + HEAD_DIM] = r_ref[:, KV_WIDTH + hd * HEAD_DIM:
                                                    KV_WIDTH + (hd + 1) * HEAD_DIM].astype(BF16)
                kv_ref[:, v0 + HEAD_DIM:v0 + 2 * HEAD_DIM] = jnp.ones((tm, HEAD_DIM), BF16)
        elif kind == _KIND_U:
            z_ref[...] = (r_before[...] * r_ref[...]).astype(BF16)
        elif kind == _KIND_GB:
            gb_ref[...] = r_ref[...].astype(BF16)

    @pl.when(t == 0)
    def _():
        for c in range(tm // _NORM_ROWS):
            norm_rows(c * _NORM_ROWS, h_cur)
        r_scr[0][...] = jnp.dot(h_cur[...], w_ref[...], preferred_element_type=F32)

    for jp in range(_NB):
        @pl.when(jnp.logical_and(t > 0, (t + _NB - 1) % _NB == jp))
        def _(jp=jp):
            r_scr[(jp + 1) % 3][...] = jnp.dot(h_cur[...], w_ref[...], preferred_element_type=F32)
            if jp < _NB - 1:
                norm_rows(jp * _NORM_ROWS, h_next)
            finish(_STEPS[jp][0], r_scr[jp % 3], r_scr[(jp - 1) % 3])
            if jp == _NB - 2:
                h_cur[...] = h_next[...]


def _in_proj(xs, mods, layer, n1g, w_in, qg, kg, cos_t, sa_t, sb_t):
    tm, tn = INPROJ_TM, INPROJ_TN
    n_tiles = N_TOKENS // tm
    lat_tiles = SEQ // tm
    assert tm % (_NB - 1) == 0 and _NORM_ROWS % 16 == 0

    def norm_tile(t):
        return jnp.where(t == 0, 0, jnp.minimum(t // _NB + 1, n_tiles - 1))

    def done(t):
        tp = jnp.maximum(t - 1, 0)
        return tp // _NB, tp % _NB

    def mod_spec(chunk):
        return pl.BlockSpec((None, None, 1, D_MODEL),
                            lambda t: (layer, _mod_row(norm_tile(t), tm), 0, chunk))

    def tab_spec():
        def index(t):
            i, _ = done(t)
            return (jnp.where(i < N_LATENT // tm, i % lat_tiles, lat_tiles), 0)
        return pl.BlockSpec((tm, HEAD_DIM), index)

    def after(kind):
        firsts = [n for n, (k, _) in enumerate(_STEPS) if k == kind]

        def index(t):
            i, j = done(t)
            col = jnp.int32(0)
            for n in firsts[1:]:
                col = col + (j >= n).astype(jnp.int32)
            return (i, col)
        return index

    return pl.pallas_call(
        _inproj_kernel,
        grid=(n_tiles * _NB + 1,),
        in_specs=[
            pl.BlockSpec((tm, D_MODEL), lambda t: (norm_tile(t), 0)),
            mod_spec(0), mod_spec(1),
            pl.BlockSpec((1, D_MODEL), lambda t: (0, 0)),
            pl.BlockSpec((D_MODEL, tn), lambda t: (0, _lookup([b for _, b in _STEPS], t % _NB))),
            pl.BlockSpec((1, HEAD_DIM), lambda t: (0, 0)),
            pl.BlockSpec((1, HEAD_DIM), lambda t: (0, 0)),
            tab_spec(), tab_spec(), tab_spec(),
        ],
        out_specs=[
            pl.BlockSpec((tm, tn), after(_KIND_Q)),
            pl.BlockSpec((tm, KV_COLS), lambda t: (done(t)[0], 0)),
            pl.BlockSpec((tm, tn), after(_KIND_U)),
            pl.BlockSpec((tm, tn), after(_KIND_GB)),
        ],
        out_shape=[
            jax.ShapeDtypeStruct((N_TOKENS, ATTN_WIDTH), BF16),
            jax.ShapeDtypeStruct((N_TOKENS, KV_COLS), BF16),
            jax.ShapeDtypeStruct((N_TOKENS, CONV_WIDTH), BF16),
            jax.ShapeDtypeStruct((N_TOKENS, CONV_WIDTH), BF16),
        ],
        scratch_shapes=[pltpu.VMEM((tm, D_MODEL), BF16)] * 2 + [pltpu.VMEM((tm, tn), F32)] * 3,
        compiler_params=_params(1),
        name="in_proj",
    )(xs, mods, mods, n1g, w_in, qg, kg, cos_t, sa_t, sb_t)


_NQ = SEQ // ATTN_TQ


def _attend(q, chunks):
    def scores(chunk):
        k_ref, _, start, size = chunk
        return lax.dot_general(q, k_ref[start:start + size, :], (((1,), (1,)), ((), ())),
                               preferred_element_type=F32)

    m = acc = None
    s = scores(chunks[0])
    for idx, (_, v_ref, start, size) in enumerate(chunks):
        s_next = scores(chunks[idx + 1]) if idx + 1 < len(chunks) else None
        m_c = jnp.max(s, axis=-1, keepdims=True)
        m_new = m_c if m is None else jnp.maximum(m, m_c)
        p = jnp.exp2(s - m_new).astype(BF16)
        pv = jnp.dot(p, v_ref[start:start + size, :], preferred_element_type=F32)
        acc = pv if m is None else jnp.exp2(m - m_new) * acc + pv
        m, s = m_new, s_next
    return acc[:, :HEAD_DIM] / acc[:, HEAD_DIM:]


def _attn_kernel(q_ref, kl_ref, vl_ref, kc_ref, vc_ref, o_ref):
    qi = pl.program_id(2)
    tq = q_ref.shape[0]
    ctx_chunk = [(kc_ref, vc_ref, 0, CTX_LEN)]
    lat_chunks = [(kl_ref, vl_ref, c * ATTN_TK, ATTN_TK) for c in range(SEQ // ATTN_TK)]

    def run(chunks):
        q = jnp.concatenate([q_ref[:, r * HEAD_DIM:(r + 1) * HEAD_DIM] for r in range(Q_PER_KV)], axis=0)
        o = _attend(q, chunks).astype(BF16)
        for r in range(Q_PER_KV):
            o_ref[:, r * HEAD_DIM:(r + 1) * HEAD_DIM] = o[r * tq:(r + 1) * tq, :]

    @pl.when(qi < _NQ)
    def _():
        run(lat_chunks + ctx_chunk)

    @pl.when(qi == _NQ)
    def _():
        run(ctx_chunk)


def _attention(q, kv):
    tq = ATTN_TQ
    ctx_blk0 = N_LATENT // CTX_LEN
    v_blk0 = KV_WIDTH // (2 * HEAD_DIM)

    def q_map(b, g, qi):
        return (jnp.where(qi < _NQ, b * _NQ + qi, N_LATENT // tq + b), g)

    return pl.pallas_call(
        _attn_kernel,
        grid=(BATCH, N_KV_HEADS, _NQ + 1),
        in_specs=[
            pl.BlockSpec((tq, Q_PER_KV * HEAD_DIM), q_map),
            pl.BlockSpec((SEQ, HEAD_DIM), lambda b, g, qi: (b, g)),
            pl.BlockSpec((SEQ, 2 * HEAD_DIM), lambda b, g, qi: (b, v_blk0 + g)),
            pl.BlockSpec((CTX_LEN, HEAD_DIM), lambda b, g, qi: (ctx_blk0 + b, g)),
            pl.BlockSpec((CTX_LEN, 2 * HEAD_DIM), lambda b, g, qi: (ctx_blk0 + b, v_blk0 + g)),
        ],
        out_specs=pl.BlockSpec((tq, Q_PER_KV * HEAD_DIM), q_map),
        out_shape=jax.ShapeDtypeStruct((N_TOKENS, ATTN_WIDTH), BF16),
        compiler_params=_params(3),
        name="attention",
    )(q, kv, kv, kv, kv)


def _merge_kernel(x_ref, attn_ref, z_ref, zp_ref, zn_ref, gb_ref, cw_ref, cb_ref, ag_ref, cg_ref, w_ref,
                  g1_ref, sh2_ref, sc2_ref, n2g_ref, xo_ref, h2_ref, cat_scr):
    tm = x_ref.shape[0]
    i = pl.program_id(0)

    row = lax.broadcasted_iota(jnp.int32, (tm, 1), 0)
    grow = i * tm + row
    seg = jnp.where(grow >= N_LATENT, CTX_LEN, SEQ)
    pos = jnp.bitwise_and(grow, seg - 1)

    z = z_ref[...].astype(F32)
    z_prev = jnp.where(row == 0, zp_ref[HALO - 1:HALO, :].astype(F32), pltpu.roll(z, 1, axis=0))
    z_prev = jnp.where(pos != 0, z_prev, 0.0)
    z_next = jnp.where(row == tm - 1, zn_ref[0:1, :].astype(F32), pltpu.roll(z, tm - 1, axis=0))
    z_next = jnp.where(pos != seg - 1, z_next, 0.0)
    conv = z_prev * cw_ref[0:1, :] + z * cw_ref[1:2, :] + z_next * cw_ref[2:3, :] + cb_ref[...]
    conv = gb_ref[...].astype(F32) * conv

    cat_scr[:, :ATTN_WIDTH] = _rms(attn_ref[...].astype(F32), ag_ref[...]).astype(BF16)
    cat_scr[:, ATTN_WIDTH:] = _rms(conv, cg_ref[...]).astype(BF16)
    y = jnp.dot(cat_scr[...], w_ref[...], preferred_element_type=F32)
    x_new = x_ref[...] + g1_ref[...] * y
    xo_ref[...] = x_new
    h2 = _rms(x_new, n2g_ref[...]) * (1.0 + sc2_ref[...]) + sh2_ref[...]
    h2_ref[...] = h2.astype(BF16)


def _merge(xs, attn, z, gb, mods, layer, cw, cb, ag, cg, w_out, n2g, n_rows):
    tm = MERGE_TM
    hb = tm // HALO
    last_hb = N_TOKENS // HALO - 1

    def mod_spec(chunk):
        return pl.BlockSpec((None, None, 1, D_MODEL), lambda i: (layer, _mod_row(i, tm), 0, chunk))

    def vec_spec(rows, width):
        return pl.BlockSpec((rows, width), lambda i: (0, 0))

    return pl.pallas_call(
        _merge_kernel,
        grid=(n_rows // tm,),
        in_specs=[
            pl.BlockSpec((tm, D_MODEL), lambda i: (i, 0)),
            pl.BlockSpec((tm, ATTN_WIDTH), lambda i: (i, 0)),
            pl.BlockSpec((tm, CONV_WIDTH), lambda i: (i, 0)),
            pl.BlockSpec((HALO, CONV_WIDTH), lambda i: (jnp.maximum(i * hb - 1, 0), 0)),
            pl.BlockSpec((HALO, CONV_WIDTH), lambda i: (jnp.minimum((i + 1) * hb, last_hb), 0)),
            pl.BlockSpec((tm, CONV_WIDTH), lambda i: (i, 0)),
            vec_spec(3, CONV_WIDTH), vec_spec(1, CONV_WIDTH),
            vec_spec(1, ATTN_WIDTH), vec_spec(1, CONV_WIDTH),
            pl.BlockSpec((D_MODEL, D_MODEL), lambda i: (0, 0)),
            mod_spec(2), mod_spec(3), mod_spec(4),
            vec_spec(1, D_MODEL),
        ],
        out_specs=[
            pl.BlockSpec((tm, D_MODEL), lambda i: (i, 0)),
            pl.BlockSpec((tm, D_MODEL), lambda i: (i, 0)),
        ],
        out_shape=[
            jax.ShapeDtypeStruct((n_rows, D_MODEL), F32),
            jax.ShapeDtypeStruct((n_rows, D_MODEL), BF16),
        ],
        scratch_shapes=[pltpu.VMEM((tm, D_MODEL), BF16)],
        compiler_params=_params(1),
        name="merge_out_proj",
    )(xs, attn, z, z, z, gb, cw, cb, ag, cg, w_out, mods, mods, mods, n2g)


def _mlp_kernel(h_ref, w1_ref, w2_ref, x_hbm, g2_ref, o_ref, x_buf, x_sem):
    i, k = pl.program_id(0), pl.program_id(1)
    tm = o_ref.shape[0]
    last_k = pl.num_programs(1) - 1

    def x_copy():
        return pltpu.make_async_copy(x_hbm.at[pl.ds(pl.multiple_of(i * tm, tm), tm), :], x_buf, x_sem)

    @pl.when(k == 0)
    def _():
        x_copy().start()
        o_ref[...] = jnp.zeros_like(o_ref)

    def up(s):
        return jnp.dot(h_ref[...], w1_ref[:, s * MLP_SUB:(s + 1) * MLP_SUB], preferred_element_type=F32)

    n_sub = w1_ref.shape[1] // MLP_SUB
    part = None
    a_next = up(0)
    for s in range(n_sub):
        a = a_next
        if s + 1 < n_sub:
            a_next = up(s + 1)
        a = jnp.maximum(a, 0.0)
        d = jnp.dot((a * a).astype(BF16), w2_ref[s * MLP_SUB:(s + 1) * MLP_SUB, :], preferred_element_type=F32)
        part = d if part is None else part + d
    o_ref[...] += part

    @pl.when(k == last_k)
    def _():
        x_copy().wait()
        o_ref[...] = x_buf[...] + g2_ref[...] * o_ref[...]


def _mlp(h2, x_new, mods, layer, w1, w2, n_rows):
    tm, th = MLP_TM, MLP_TH
    return pl.pallas_call(
        _mlp_kernel,
        grid=(n_rows // tm, MLP_HIDDEN // th),
        in_specs=[
            pl.BlockSpec((tm, D_MODEL), lambda i, k: (i, 0)),
            pl.BlockSpec((D_MODEL, th), lambda i, k: (0, k)),
            pl.BlockSpec((th, D_MODEL), lambda i, k: (k, 0)),
            pl.BlockSpec(memory_space=pl.ANY),
            pl.BlockSpec((None, None, 1, D_MODEL), lambda i, k: (layer, _mod_row(i, tm), 0, 5)),
        ],
        out_specs=pl.BlockSpec((tm, D_MODEL), lambda i, k: (i, 0)),
        out_shape=jax.ShapeDtypeStruct((n_rows, D_MODEL), F32),
        scratch_shapes=[pltpu.VMEM((tm, D_MODEL), F32), pltpu.SemaphoreType.DMA(())],
        compiler_params=_params(2),
        name="mlp",
    )(h2, w1, w2, x_new, mods)


def _rope_tables(tm):
    t = jnp.arange(SEQ)
    row = (t // GRID_W).astype(F32)
    col = (t % GRID_W).astype(F32)
    inv_freq = ROPE_THETA ** (-jnp.arange(0, AXIS_DIM, 2, dtype=F32) / AXIS_DIM)
    ang_r = row[:, None] * inv_freq[None, :]
    ang_c = col[:, None] * inv_freq[None, :]
    ang = jnp.concatenate([ang_r, ang_r, ang_c, ang_c], axis=-1)
    cos, sin = jnp.cos(ang), jnp.sin(ang)
    first = (jnp.arange(HEAD_DIM) % AXIS_DIM) < AXIS_DIM // 2
    sin_a = jnp.where(first, -sin, 0.0)
    sin_b = jnp.where(first, 0.0, sin)
    ident = jnp.zeros((tm, HEAD_DIM), F32)
    return (jnp.concatenate([cos, ident + 1.0]), jnp.concatenate([sin_a, ident]),
            jnp.concatenate([sin_b, ident]))


def kernel(x, c, ctx, c_ctx, w_ada, b_ada, norm1_g, w_in, q_norm_g, k_norm_g, conv_w, conv_b,
           attn_out_g, conv_out_g, w_out, norm2_g, w_mlp_in, w_mlp_out):
    assert x.shape == (BATCH, SEQ, D_MODEL) and ctx.shape == (BATCH, CTX_LEN, D_MODEL)
    xs = jnp.concatenate([x.reshape(N_LATENT, D_MODEL), ctx.reshape(N_CTX, D_MODEL)], axis=0)
    cond = jnp.concatenate([c, c_ctx[None, :], jnp.zeros((MOD_ROWS - BATCH - 1, D_MODEL), F32)], axis=0)
    mods = _adaln_all(cond, w_ada, b_ada).reshape(DEPTH, MOD_ROWS, 1, N_MOD * D_MODEL)
    cos_t, sa_t, sb_t = _rope_tables(INPROJ_TM)

    for l in range(DEPTH):
        q, kv, z, gb = _in_proj(xs, mods, l, norm1_g[l][None], w_in[l].astype(BF16),
                                q_norm_g[l][None], k_norm_g[l][None], cos_t, sa_t, sb_t)
        attn = _attention(q, kv)
        n_rows = N_TOKENS if l < DEPTH - 1 else N_LATENT
        x_new, h2 = _merge(xs, attn, z, gb, mods, l, conv_w[l], conv_b[l][None], attn_out_g[l][None],
                           conv_out_g[l][None], w_out[l].astype(BF16), norm2_g[l][None], n_rows)
        xs = _mlp(h2, x_new, mods, l, w_mlp_in[l].astype(BF16), w_mlp_out[l].astype(BF16), n_rows)
    return xs.reshape(BATCH, SEQ, D_MODEL)
```

```python
import jax
import jax.numpy as jnp
from jax import lax
from jax.experimental import pallas as pl
from jax.experimental.pallas import tpu as pltpu

D_MODEL = 2048
BATCH = 4
SEQ = 4096
DEPTH = 4
CTX_LEN = 256
GRID_W = 64
HEAD_DIM = 128
N_HEADS = 8
N_KV_HEADS = 2
Q_PER_KV = N_HEADS // N_KV_HEADS
ATTN_WIDTH = N_HEADS * HEAD_DIM
KV_WIDTH = N_KV_HEADS * HEAD_DIM
CONV_WIDTH = D_MODEL - ATTN_WIDTH
IN_WIDTH = ATTN_WIDTH + 2 * KV_WIDTH + 3 * CONV_WIDTH
MLP_HIDDEN = 4 * D_MODEL
N_MOD = 6
ROPE_THETA = 10000.0
AXIS_DIM = HEAD_DIM // 2
EPS = 1e-6
LOG2_E = 1.4426950408889634
KV_COLS = KV_WIDTH + 2 * KV_WIDTH

N_LATENT = BATCH * SEQ
N_CTX = BATCH * CTX_LEN
N_TOKENS = N_LATENT + N_CTX
MOD_ROWS = 8

F32 = jnp.float32
BF16 = jnp.bfloat16

ADA_TN = 1024
INPROJ_TM = 1024
INPROJ_TN = 512
ATTN_TQ = 512
ATTN_TK = 512
MERGE_TM = 256
MLP_TM = 512
MLP_TH = 2048
MLP_SUB = 512
HALO = 16
VMEM_LIMIT = 56 * 1024 * 1024


def _params(n_axes, vmem=VMEM_LIMIT):
    return pltpu.CompilerParams(dimension_semantics=("arbitrary",) * n_axes, vmem_limit_bytes=vmem)


def _mod_row(tile, tm):
    return jnp.minimum(tile // (SEQ // tm), BATCH)


def _rms(x, g):
    return x * lax.rsqrt(jnp.mean(x * x, axis=-1, keepdims=True) + EPS) * g


def _ada_kernel(cond_ref, w_ref, b_ref, o_ref):
    cnd = cond_ref[...]
    s = (cnd * jax.nn.sigmoid(cnd)).astype(BF16)
    w = w_ref[...].astype(BF16)
    o_ref[...] = jnp.dot(s, w, preferred_element_type=F32) + b_ref[...]


def _adaln_all(cond, w_ada, b_ada):
    n = N_MOD * D_MODEL
    return pl.pallas_call(
        _ada_kernel,
        grid=(DEPTH, n // ADA_TN),
        in_specs=[
            pl.BlockSpec((MOD_ROWS, D_MODEL), lambda l, j: (0, 0)),
            pl.BlockSpec((None, D_MODEL, ADA_TN), lambda l, j: (l, 0, j)),
            pl.BlockSpec((None, 1, ADA_TN), lambda l, j: (l, 0, j)),
        ],
        out_specs=pl.BlockSpec((None, MOD_ROWS, ADA_TN), lambda l, j: (l, 0, j)),
        out_shape=jax.ShapeDtypeStruct((DEPTH, MOD_ROWS, n), F32),
        compiler_params=_params(2),
        name="adaln",
    )(cond, w_ada, b_ada.reshape(DEPTH, 1, n))


_KIND_Q, _KIND_KV, _KIND_GC, _KIND_U, _KIND_GB = range(5)
_QB = ATTN_WIDTH // INPROJ_TN
_CB = CONV_WIDTH // INPROJ_TN
_GB0 = (ATTN_WIDTH + 2 * KV_WIDTH) // INPROJ_TN
_STEPS = ([(_KIND_Q, b) for b in range(_QB)] + [(_KIND_KV, _QB)]
          + [kb for c in range(_CB) for kb in ((_KIND_GC, _GB0 + _CB + c), (_KIND_U, _GB0 + 2 * _CB + c))]
          + [(_KIND_GB, _GB0 + c) for c in range(_CB)])
_NB = len(_STEPS)
_NORM_ROWS = INPROJ_TM // (_NB - 1)


def _lookup(values, idx):
    out = jnp.int32(values[0])
    for n, v in enumerate(values[1:], start=1):
        out = jnp.where(idx == n, jnp.int32(v), out)
    return out


def _rope(x, cos, sin_a, sin_b):
    return (x * cos + pltpu.roll(x, HEAD_DIM - AXIS_DIM // 2, axis=1) * sin_a
            + pltpu.roll(x, AXIS_DIM // 2, axis=1) * sin_b)


def _inproj_kernel(x_ref, sh_ref, sc_ref, n1g_ref, w_ref, qg_ref, kg_ref, cos_ref, sa_ref, sb_ref,
                   q_ref, kv_ref, z_ref, gb_ref, h_cur, h_next, r0_scr, r1_scr, r2_scr):
    t = pl.program_id(0)
    tm = x_ref.shape[0]
    r_scr = (r0_scr, r1_scr, r2_scr)
    assert _NB % len(r_scr) == 0

    def norm_rows(r0, dst):
        x = x_ref[r0:r0 + _NORM_ROWS, :]
        h = _rms(x, n1g_ref[...]) * (1.0 + sc_ref[...]) + sh_ref[...]
        dst[r0:r0 + _NORM_ROWS, :] = h.astype(BF16)

    def head(r_ref, col0, gain, scale):
        y = _rms(r_ref[:, col0:col0 + HEAD_DIM], gain)
        y = _rope(y, cos_ref[...], sa_ref[...], sb_ref[...])
        return (y * scale).astype(BF16)

    def finish(kind, r_ref, r_before):
        if kind == _KIND_Q:
            for hd in range(INPROJ_TN // HEAD_DIM):
                q_ref[:, hd * HEAD_DIM:(hd + 1) * HEAD_DIM] = head(r_ref, hd * HEAD_DIM, qg_ref[...],
                                                                   HEAD_DIM ** -0.5 * LOG2_E)
        elif kind == _KIND_KV:
            for hd in range(N_KV_HEADS):
                kv_ref[:, hd * HEAD_DIM:(hd + 1) * HEAD_DIM] = head(r_ref, hd * HEAD_DIM, kg_ref[...], 1.0)
                v0 = KV_WIDTH + hd * 2 * HEAD_DIM
                kv_ref[:, v0:v0 + HEAD_DIM] = r_ref[:, KV_WIDTH + hd * HEAD_DIM:
                                                    KV_WIDTH + (hd + 1) * HEAD_DIM].astype(BF16)
                kv_ref[:, v0 + HEAD_DIM:v0 + 2 * HEAD_DIM] = jnp.ones((tm, HEAD_DIM), BF16)
        elif kind == _KIND_U:
            z_ref[...] = (r_before[...] * r_ref[...]).astype(BF16)
        elif kind == _KIND_GB:
            gb_ref[...] = r_ref[...].astype(BF16)

    @pl.when(t == 0)
    def _():
        for c in range(tm // _NORM_ROWS):
            norm_rows(c * _NORM_ROWS, h_cur)
        r_scr[0][...] = jnp.dot(h_cur[...], w_ref[...], preferred_element_type=F32)

    for jp in range(_NB):
        @pl.when(jnp.logical_and(t > 0, (t + _NB - 1) % _NB == jp))
        def _(jp=jp):
            r_scr[(jp + 1) % 3][...] = jnp.dot(h_cur[...], w_ref[...], preferred_element_type=F32)
            if jp < _NB - 1:
                norm_rows(jp * _NORM_ROWS, h_next)
            finish(_STEPS[jp][0], r_scr[jp % 3], r_scr[(jp - 1) % 3])
            if jp == _NB - 2:
                h_cur[...] = h_next[...]


def _in_proj(xs, mods, layer, n1g, w_in, qg, kg, cos_t, sa_t, sb_t):
    tm, tn = INPROJ_TM, INPROJ_TN
    n_tiles = N_TOKENS // tm
    lat_tiles = SEQ // tm
    assert tm % (_NB - 1) == 0 and _NORM_ROWS % 16 == 0

    def norm_tile(t):
        return jnp.where(t == 0, 0, jnp.minimum(t // _NB + 1, n_tiles - 1))

    def done(t):
        tp = jnp.maximum(t - 1, 0)
        return tp // _NB, tp % _NB

    def mod_spec(chunk):
        return pl.BlockSpec((None, None, 1, D_MODEL),
                            lambda t: (layer, _mod_row(norm_tile(t), tm), 0, chunk))

    def tab_spec():
        def index(t):
            i, _ = done(t)
            return (jnp.where(i < N_LATENT // tm, i % lat_tiles, lat_tiles), 0)
        return pl.BlockSpec((tm, HEAD_DIM), index)

    def after(kind):
        firsts = [n for n, (k, _) in enumerate(_STEPS) if k == kind]

        def index(t):
            i, j = done(t)
            col = jnp.int32(0)
            for n in firsts[1:]:
                col = col + (j >= n).astype(jnp.int32)
            return (i, col)
        return index

    return pl.pallas_call(
        _inproj_kernel,
        grid=(n_tiles * _NB + 1,),
        in_specs=[
            pl.BlockSpec((tm, D_MODEL), lambda t: (norm_tile(t), 0)),
            mod_spec(0), mod_spec(1),
            pl.BlockSpec((1, D_MODEL), lambda t: (0, 0)),
            pl.BlockSpec((D_MODEL, tn), lambda t: (0, _lookup([b for _, b in _STEPS], t % _NB))),
            pl.BlockSpec((1, HEAD_DIM), lambda t: (0, 0)),
            pl.BlockSpec((1, HEAD_DIM), lambda t: (0, 0)),
            tab_spec(), tab_spec(), tab_spec(),
        ],
        out_specs=[
            pl.BlockSpec((tm, tn), after(_KIND_Q)),
            pl.BlockSpec((tm, KV_COLS), lambda t: (done(t)[0], 0)),
            pl.BlockSpec((tm, tn), after(_KIND_U)),
            pl.BlockSpec((tm, tn), after(_KIND_GB)),
        ],
        out_shape=[
            jax.ShapeDtypeStruct((N_TOKENS, ATTN_WIDTH), BF16),
            jax.ShapeDtypeStruct((N_TOKENS, KV_COLS), BF16),
            jax.ShapeDtypeStruct((N_TOKENS, CONV_WIDTH), BF16),
            jax.ShapeDtypeStruct((N_TOKENS, CONV_WIDTH), BF16),
        ],
        scratch_shapes=[pltpu.VMEM((tm, D_MODEL), BF16)] * 2 + [pltpu.VMEM((tm, tn), F32)] * 3,
        compiler_params=_params(1),
        name="in_proj",
    )(xs, mods, mods, n1g, w_in, qg, kg, cos_t, sa_t, sb_t)


_NQ = SEQ // ATTN_TQ


def _attend(q, chunks):
    def scores(chunk):
        k_ref, _, start, size = chunk
        return lax.dot_general(q, k_ref[start:start + size, :], (((1,), (1,)), ((), ())),
                               preferred_element_type=F32)

    m = acc = None
    s = scores(chunks[0])
    for idx, (_, v_ref, start, size) in enumerate(chunks):
        s_next = scores(chunks[idx + 1]) if idx + 1 < len(chunks) else None
        m_c = jnp.max(s, axis=-1, keepdims=True)
        m_new = m_c if m is None else jnp.maximum(m, m_c)
        p = jnp.exp2(s - m_new).astype(BF16)
        pv = jnp.dot(p, v_ref[start:start + size, :], preferred_element_type=F32)
        acc = pv if m is None else jnp.exp2(m - m_new) * acc + pv
        m, s = m_new, s_next
    return acc[:, :HEAD_DIM] / acc[:, HEAD_DIM:]


def _attend_heads(q_ref, o_ref, chunks):
    tq = q_ref.shape[0]
    q = jnp.concatenate([q_ref[:, r * HEAD_DIM:(r + 1) * HEAD_DIM] for r in range(Q_PER_KV)], axis=0)
    o = _attend(q, chunks).astype(BF16)
    for r in range(Q_PER_KV):
        o_ref[:, r * HEAD_DIM:(r + 1) * HEAD_DIM] = o[r * tq:(r + 1) * tq, :]


def _attn_latent_kernel(q_ref, kl_ref, vl_ref, kc_ref, vc_ref, o_ref):
    chunks = [(kl_ref, vl_ref, c * ATTN_TK, ATTN_TK) for c in range(SEQ // ATTN_TK)]
    _attend_heads(q_ref, o_ref, chunks + [(kc_ref, vc_ref, 0, CTX_LEN)])


def _attn_context_kernel(q_ref, kc_ref, vc_ref, o_ref):
    _attend_heads(q_ref, o_ref, [(kc_ref, vc_ref, 0, CTX_LEN)])


def _attention(q, kv):
    tq = ATTN_TQ
    ctx_blk0 = N_LATENT // CTX_LEN
    v_blk0 = KV_WIDTH // (2 * HEAD_DIM)
    width = Q_PER_KV * HEAD_DIM

    attn = pl.pallas_call(
        _attn_latent_kernel,
        grid=(BATCH, N_KV_HEADS, _NQ),
        in_specs=[
            pl.BlockSpec((tq, width), lambda b, g, qi: (b * _NQ + qi, g)),
            pl.BlockSpec((SEQ, HEAD_DIM), lambda b, g, qi: (b, g)),
            pl.BlockSpec((SEQ, 2 * HEAD_DIM), lambda b, g, qi: (b, v_blk0 + g)),
            pl.BlockSpec((CTX_LEN, HEAD_DIM), lambda b, g, qi: (ctx_blk0 + b, g)),
            pl.BlockSpec((CTX_LEN, 2 * HEAD_DIM), lambda b, g, qi: (ctx_blk0 + b, v_blk0 + g)),
        ],
        out_specs=pl.BlockSpec((tq, width), lambda b, g, qi: (b * _NQ + qi, g)),
        out_shape=jax.ShapeDtypeStruct((N_LATENT, ATTN_WIDTH), BF16),
        compiler_params=_params(3),
        name="attention",
    )(q, kv, kv, kv, kv)
    attn_ctx = pl.pallas_call(
        _attn_context_kernel,
        grid=(BATCH, N_KV_HEADS),
        in_specs=[
            pl.BlockSpec((CTX_LEN, width), lambda b, g: (ctx_blk0 + b, g)),
            pl.BlockSpec((CTX_LEN, HEAD_DIM), lambda b, g: (ctx_blk0 + b, g)),
            pl.BlockSpec((CTX_LEN, 2 * HEAD_DIM), lambda b, g: (ctx_blk0 + b, v_blk0 + g)),
        ],
        out_specs=pl.BlockSpec((CTX_LEN, width), lambda b, g: (b, g)),
        out_shape=jax.ShapeDtypeStruct((N_CTX, ATTN_WIDTH), BF16),
        compiler_params=_params(2),
        name="attention_ctx",
    )(q, kv, kv)
    return attn, attn_ctx


def _merge_kernel(x_ref, attn_lat_ref, attn_ctx_ref, z_ref, zp_ref, zn_ref, gb_ref, cw_ref, cb_ref, ag_ref,
                  cg_ref, w_ref, g1_ref, sh2_ref, sc2_ref, n2g_ref, xo_ref, h2_ref, cat_scr):
    tm = x_ref.shape[0]
    i = pl.program_id(0)
    attn = jnp.where(i * tm >= N_LATENT, attn_ctx_ref[...], attn_lat_ref[...])

    row = lax.broadcasted_iota(jnp.int32, (tm, 1), 0)
    grow = i * tm + row
    seg = jnp.where(grow >= N_LATENT, CTX_LEN, SEQ)
    pos = jnp.bitwise_and(grow, seg - 1)

    z = z_ref[...].astype(F32)
    z_prev = jnp.where(row == 0, zp_ref[HALO - 1:HALO, :].astype(F32), pltpu.roll(z, 1, axis=0))
    z_prev = jnp.where(pos != 0, z_prev, 0.0)
    z_next = jnp.where(row == tm - 1, zn_ref[0:1, :].astype(F32), pltpu.roll(z, tm - 1, axis=0))
    z_next = jnp.where(pos != seg - 1, z_next, 0.0)
    conv = z_prev * cw_ref[0:1, :] + z * cw_ref[1:2, :] + z_next * cw_ref[2:3, :] + cb_ref[...]
    conv = gb_ref[...].astype(F32) * conv

    cat_scr[:, :ATTN_WIDTH] = _rms(attn.astype(F32), ag_ref[...]).astype(BF16)
    cat_scr[:, ATTN_WIDTH:] = _rms(conv, cg_ref[...]).astype(BF16)
    y = jnp.dot(cat_scr[...], w_ref[...], preferred_element_type=F32)
    x_new = x_ref[...] + g1_ref[...] * y
    xo_ref[...] = x_new
    h2 = _rms(x_new, n2g_ref[...]) * (1.0 + sc2_ref[...]) + sh2_ref[...]
    h2_ref[...] = h2.astype(BF16)


def _merge(xs, attn_lat, attn_ctx, z, gb, mods, layer, cw, cb, ag, cg, w_out, n2g, n_rows):
    tm = MERGE_TM
    hb = tm // HALO
    last_hb = N_TOKENS // HALO - 1
    lat_tiles = N_LATENT // tm
    assert N_LATENT % tm == 0 and N_CTX % tm == 0

    def mod_spec(chunk):
        return pl.BlockSpec((None, None, 1, D_MODEL), lambda i: (layer, _mod_row(i, tm), 0, chunk))

    def vec_spec(rows, width):
        return pl.BlockSpec((rows, width), lambda i: (0, 0))

    return pl.pallas_call(
        _merge_kernel,
        grid=(n_rows // tm,),
        in_specs=[
            pl.BlockSpec((tm, D_MODEL), lambda i: (i, 0)),
            pl.BlockSpec((tm, ATTN_WIDTH), lambda i: (jnp.minimum(i, lat_tiles - 1), 0)),
            pl.BlockSpec((tm, ATTN_WIDTH), lambda i: (jnp.maximum(i - lat_tiles, 0), 0)),
            pl.BlockSpec((tm, CONV_WIDTH), lambda i: (i, 0)),
            pl.BlockSpec((HALO, CONV_WIDTH), lambda i: (jnp.maximum(i * hb - 1, 0), 0)),
            pl.BlockSpec((HALO, CONV_WIDTH), lambda i: (jnp.minimum((i + 1) * hb, last_hb), 0)),
            pl.BlockSpec((tm, CONV_WIDTH), lambda i: (i, 0)),
            vec_spec(3, CONV_WIDTH), vec_spec(1, CONV_WIDTH),
            vec_spec(1, ATTN_WIDTH), vec_spec(1, CONV_WIDTH),
            pl.BlockSpec((D_MODEL, D_MODEL), lambda i: (0, 0)),
            mod_spec(2), mod_spec(3), mod_spec(4),
            vec_spec(1, D_MODEL),
        ],
        out_specs=[
            pl.BlockSpec((tm, D_MODEL), lambda i: (i, 0)),
            pl.BlockSpec((tm, D_MODEL), lambda i: (i, 0)),
        ],
        out_shape=[
            jax.ShapeDtypeStruct((n_rows, D_MODEL), F32),
            jax.ShapeDtypeStruct((n_rows, D_MODEL), BF16),
        ],
        scratch_shapes=[pltpu.VMEM((tm, D_MODEL), BF16)],
        compiler_params=_params(1),
        name="merge_out_proj",
    )(xs, attn_lat, attn_ctx, z, z, z, gb, cw, cb, ag, cg, w_out, mods, mods, mods, n2g)


def _mlp_kernel(h_ref, w1_ref, w2_ref, x_hbm, g2_ref, o_ref, x_buf, x_sem):
    i, k = pl.program_id(0), pl.program_id(1)
    tm = o_ref.shape[0]
    last_k = pl.num_programs(1) - 1

    def x_copy():
        return pltpu.make_async_copy(x_hbm.at[pl.ds(pl.multiple_of(i * tm, tm), tm), :], x_buf, x_sem)

    @pl.when(k == 0)
    def _():
        x_copy().start()
        o_ref[...] = jnp.zeros_like(o_ref)

    def up(s):
        return jnp.dot(h_ref[...], w1_ref[:, s * MLP_SUB:(s + 1) * MLP_SUB], preferred_element_type=F32)

    n_sub = w1_ref.shape[1] // MLP_SUB
    part = None
    a_next = up(0)
    for s in range(n_sub):
        a = a_next
        if s + 1 < n_sub:
            a_next = up(s + 1)
        a = jnp.maximum(a, 0.0)
        d = jnp.dot((a * a).astype(BF16), w2_ref[s * MLP_SUB:(s + 1) * MLP_SUB, :], preferred_element_type=F32)
        part = d if part is None else part + d
    o_ref[...] += part

    @pl.when(k == last_k)
    def _():
        x_copy().wait()
        o_ref[...] = x_buf[...] + g2_ref[...] * o_ref[...]


def _mlp(h2, x_new, mods, layer, w1, w2, n_rows):
    tm, th = MLP_TM, MLP_TH
    return pl.pallas_call(
        _mlp_kernel,
        grid=(n_rows // tm, MLP_HIDDEN // th),
        in_specs=[
            pl.BlockSpec((tm, D_MODEL), lambda i, k: (i, 0)),
            pl.BlockSpec((D_MODEL, th), lambda i, k: (0, k)),
            pl.BlockSpec((th, D_MODEL), lambda i, k: (k, 0)),
            pl.BlockSpec(memory_space=pl.ANY),
            pl.BlockSpec((None, None, 1, D_MODEL), lambda i, k: (layer, _mod_row(i, tm), 0, 5)),
        ],
        out_specs=pl.BlockSpec((tm, D_MODEL), lambda i, k: (i, 0)),
        out_shape=jax.ShapeDtypeStruct((n_rows, D_MODEL), F32),
        scratch_shapes=[pltpu.VMEM((tm, D_MODEL), F32), pltpu.SemaphoreType.DMA(())],
        compiler_params=_params(2),
        name="mlp",
    )(h2, w1, w2, x_new, mods)


def _rope_tables(tm):
    t = jnp.arange(SEQ)
    row = (t // GRID_W).astype(F32)
    col = (t % GRID_W).astype(F32)
    inv_freq = ROPE_THETA ** (-jnp.arange(0, AXIS_DIM, 2, dtype=F32) / AXIS_DIM)
    ang_r = row[:, None] * inv_freq[None, :]
    ang_c = col[:, None] * inv_freq[None, :]
    ang = jnp.concatenate([ang_r, ang_r, ang_c, ang_c], axis=-1)
    cos, sin = jnp.cos(ang), jnp.sin(ang)
    first = (jnp.arange(HEAD_DIM) % AXIS_DIM) < AXIS_DIM // 2
    sin_a = jnp.where(first, -sin, 0.0)
    sin_b = jnp.where(first, 0.0, sin)
    ident = jnp.zeros((tm, HEAD_DIM), F32)
    return (jnp.concatenate([cos, ident + 1.0]), jnp.concatenate([sin_a, ident]),
            jnp.concatenate([sin_b, ident]))


def kernel(x, c, ctx, c_ctx, w_ada, b_ada, norm1_g, w_in, q_norm_g, k_norm_g, conv_w, conv_b,
           attn_out_g, conv_out_g, w_out, norm2_g, w_mlp_in, w_mlp_out):
    assert x.shape == (BATCH, SEQ, D_MODEL) and ctx.shape == (BATCH, CTX_LEN, D_MODEL)
    xs = jnp.concatenate([x.reshape(N_LATENT, D_MODEL), ctx.reshape(N_CTX, D_MODEL)], axis=0)
    cond = jnp.concatenate([c, c_ctx[None, :], jnp.zeros((MOD_ROWS - BATCH - 1, D_MODEL), F32)], axis=0)
    mods = _adaln_all(cond, w_ada, b_ada).reshape(DEPTH, MOD_ROWS, 1, N_MOD * D_MODEL)
    cos_t, sa_t, sb_t = _rope_tables(INPROJ_TM)

    for l in range(DEPTH):
        q, kv, z, gb = _in_proj(xs, mods, l, norm1_g[l][None], w_in[l].astype(BF16),
                                q_norm_g[l][None], k_norm_g[l][None], cos_t, sa_t, sb_t)
        attn_lat, attn_ctx = _attention(q, kv)
        n_rows = N_TOKENS if l < DEPTH - 1 else N_LATENT
        x_new, h2 = _merge(xs, attn_lat, attn_ctx, z, gb, mods, l, conv_w[l], conv_b[l][None], attn_out_g[l][None],
                           conv_out_g[l][None], w_out[l].astype(BF16), norm2_g[l][None], n_rows)
        xs = _mlp(h2, x_new, mods, l, w_mlp_in[l].astype(BF16), w_mlp_out[l].astype(BF16), n_rows)
    return xs.reshape(BATCH, SEQ, D_MODEL)
```

```python
import jax
import jax.numpy as jnp
from jax import lax
from jax.experimental import pallas as pl
from jax.experimental.pallas import tpu as pltpu

D_MODEL = 2048
BATCH = 4
SEQ = 4096
DEPTH = 4
CTX_LEN = 256
GRID_W = 64
HEAD_DIM = 128
N_HEADS = 8
N_KV_HEADS = 2
Q_PER_KV = N_HEADS // N_KV_HEADS
ATTN_WIDTH = N_HEADS * HEAD_DIM
KV_WIDTH = N_KV_HEADS * HEAD_DIM
CONV_WIDTH = D_MODEL - ATTN_WIDTH
IN_WIDTH = ATTN_WIDTH + 2 * KV_WIDTH + 3 * CONV_WIDTH
MLP_HIDDEN = 4 * D_MODEL
N_MOD = 6
ROPE_THETA = 10000.0
AXIS_DIM = HEAD_DIM // 2
EPS = 1e-6
LOG2_E = 1.4426950408889634
KV_COLS = KV_WIDTH + 2 * KV_WIDTH

N_LATENT = BATCH * SEQ
N_CTX = BATCH * CTX_LEN
N_TOKENS = N_LATENT + N_CTX
MOD_ROWS = 8

F32 = jnp.float32
BF16 = jnp.bfloat16

ADA_TN = 1024
INPROJ_TM = 1024
INPROJ_TN = 512
ATTN_TQ = 512
ATTN_TK = 512
MERGE_TM = 256
MLP_TM = 1024
MLP_TH = 1024
MLP_SUB = 512
MLP_OUT_SLAB = 512
HALO = 16
VMEM_LIMIT = 56 * 1024 * 1024


def _params(n_axes, vmem=VMEM_LIMIT):
    return pltpu.CompilerParams(dimension_semantics=("arbitrary",) * n_axes, vmem_limit_bytes=vmem)


def _mod_row(tile, tm):
    return jnp.minimum(tile // (SEQ // tm), BATCH)


def _rms(x, g):
    return x * lax.rsqrt(jnp.mean(x * x, axis=-1, keepdims=True) + EPS) * g


def _ada_kernel(cond_ref, w_ref, b_ref, o_ref):
    cnd = cond_ref[...]
    s = (cnd * jax.nn.sigmoid(cnd)).astype(BF16)
    w = w_ref[...].astype(BF16)
    o_ref[...] = jnp.dot(s, w, preferred_element_type=F32) + b_ref[...]


def _adaln_all(cond, w_ada, b_ada):
    n = N_MOD * D_MODEL
    return pl.pallas_call(
        _ada_kernel,
        grid=(DEPTH, n // ADA_TN),
        in_specs=[
            pl.BlockSpec((MOD_ROWS, D_MODEL), lambda l, j: (0, 0)),
            pl.BlockSpec((None, D_MODEL, ADA_TN), lambda l, j: (l, 0, j)),
            pl.BlockSpec((None, 1, ADA_TN), lambda l, j: (l, 0, j)),
        ],
        out_specs=pl.BlockSpec((None, MOD_ROWS, ADA_TN), lambda l, j: (l, 0, j)),
        out_shape=jax.ShapeDtypeStruct((DEPTH, MOD_ROWS, n), F32),
        compiler_params=_params(2),
        name="adaln",
    )(cond, w_ada, b_ada.reshape(DEPTH, 1, n))


_KIND_Q, _KIND_KV, _KIND_GC, _KIND_U, _KIND_GB = range(5)
_QB = ATTN_WIDTH // INPROJ_TN
_CB = CONV_WIDTH // INPROJ_TN
_GB0 = (ATTN_WIDTH + 2 * KV_WIDTH) // INPROJ_TN
_STEPS = ([(_KIND_Q, b) for b in range(_QB)] + [(_KIND_KV, _QB)]
          + [kb for c in range(_CB) for kb in ((_KIND_GC, _GB0 + _CB + c), (_KIND_U, _GB0 + 2 * _CB + c))]
          + [(_KIND_GB, _GB0 + c) for c in range(_CB)])
_NB = len(_STEPS)
_NORM_ROWS = INPROJ_TM // (_NB - 1)


def _lookup(values, idx):
    out = jnp.int32(values[0])
    for n, v in enumerate(values[1:], start=1):
        out = jnp.where(idx == n, jnp.int32(v), out)
    return out


def _rope(x, cos, sin_a, sin_b):
    return (x * cos + pltpu.roll(x, HEAD_DIM - AXIS_DIM // 2, axis=1) * sin_a
            + pltpu.roll(x, AXIS_DIM // 2, axis=1) * sin_b)


def _inproj_kernel(x_ref, sh_ref, sc_ref, n1g_ref, w_ref, qg_ref, kg_ref, cos_ref, sa_ref, sb_ref,
                   q_ref, kv_ref, z_ref, gb_ref, h_cur, h_next, r0_scr, r1_scr, r2_scr):
    t = pl.program_id(0)
    tm = x_ref.shape[0]
    r_scr = (r0_scr, r1_scr, r2_scr)
    assert _NB % len(r_scr) == 0

    def norm_rows(r0, dst):
        x = x_ref[r0:r0 + _NORM_ROWS, :]
        h = _rms(x, n1g_ref[...]) * (1.0 + sc_ref[...]) + sh_ref[...]
        dst[r0:r0 + _NORM_ROWS, :] = h.astype(BF16)

    def head(r_ref, col0, gain, scale):
        y = _rms(r_ref[:, col0:col0 + HEAD_DIM], gain)
        y = _rope(y, cos_ref[...], sa_ref[...], sb_ref[...])
        return (y * scale).astype(BF16)

    def finish(kind, r_ref, r_before):
        if kind == _KIND_Q:
            for hd in range(INPROJ_TN // HEAD_DIM):
                q_ref[:, hd * HEAD_DIM:(hd + 1) * HEAD_DIM] = head(r_ref, hd * HEAD_DIM, qg_ref[...],
                                                                   HEAD_DIM ** -0.5 * LOG2_E)
        elif kind == _KIND_KV:
            for hd in range(N_KV_HEADS):
                kv_ref[:, hd * HEAD_DIM:(hd + 1) * HEAD_DIM] = head(r_ref, hd * HEAD_DIM, kg_ref[...], 1.0)
                v0 = KV_WIDTH + hd * 2 * HEAD_DIM
                kv_ref[:, v0:v0 + HEAD_DIM] = r_ref[:, KV_WIDTH + hd * HEAD_DIM:
                                                    KV_WIDTH + (hd + 1) * HEAD_DIM].astype(BF16)
                kv_ref[:, v0 + HEAD_DIM:v0 + 2 * HEAD_DIM] = jnp.ones((tm, HEAD_DIM), BF16)
        elif kind == _KIND_U:
            z_ref[...] = (r_before[...] * r_ref[...]).astype(BF16)
        elif kind == _KIND_GB:
            gb_ref[...] = r_ref[...].astype(BF16)

    @pl.when(t == 0)
    def _():
        for c in range(tm // _NORM_ROWS):
            norm_rows(c * _NORM_ROWS, h_cur)
        r_scr[0][...] = jnp.dot(h_cur[...], w_ref[...], preferred_element_type=F32)

    for jp in range(_NB):
        @pl.when(jnp.logical_and(t > 0, (t + _NB - 1) % _NB == jp))
        def _(jp=jp):
            r_scr[(jp + 1) % 3][...] = jnp.dot(h_cur[...], w_ref[...], preferred_element_type=F32)
            if jp < _NB - 1:
                norm_rows(jp * _NORM_ROWS, h_next)
            finish(_STEPS[jp][0], r_scr[jp % 3], r_scr[(jp - 1) % 3])
            if jp == _NB - 2:
                h_cur[...] = h_next[...]


def _in_proj(xs, mods, layer, n1g, w_in, qg, kg, cos_t, sa_t, sb_t):
    tm, tn = INPROJ_TM, INPROJ_TN
    n_tiles = N_TOKENS // tm
    lat_tiles = SEQ // tm
    assert tm % (_NB - 1) == 0 and _NORM_ROWS % 16 == 0

    def norm_tile(t):
        return jnp.where(t == 0, 0, jnp.minimum(t // _NB + 1, n_tiles - 1))

    def done(t):
        tp = jnp.maximum(t - 1, 0)
        return tp // _NB, tp % _NB

    def mod_spec(chunk):
        return pl.BlockSpec((None, None, 1, D_MODEL),
                            lambda t: (layer, _mod_row(norm_tile(t), tm), 0, chunk))

    def tab_spec():
        def index(t):
            i, _ = done(t)
            return (jnp.where(i < N_LATENT // tm, i % lat_tiles, lat_tiles), 0)
        return pl.BlockSpec((tm, HEAD_DIM), index)

    def after(kind):
        firsts = [n for n, (k, _) in enumerate(_STEPS) if k == kind]

        def index(t):
            i, j = done(t)
            col = jnp.int32(0)
            for n in firsts[1:]:
                col = col + (j >= n).astype(jnp.int32)
            return (i, col)
        return index

    return pl.pallas_call(
        _inproj_kernel,
        grid=(n_tiles * _NB + 1,),
        in_specs=[
            pl.BlockSpec((tm, D_MODEL), lambda t: (norm_tile(t), 0)),
            mod_spec(0), mod_spec(1),
            pl.BlockSpec((1, D_MODEL), lambda t: (0, 0)),
            pl.BlockSpec((D_MODEL, tn), lambda t: (0, _lookup([b for _, b in _STEPS], t % _NB))),
            pl.BlockSpec((1, HEAD_DIM), lambda t: (0, 0)),
            pl.BlockSpec((1, HEAD_DIM), lambda t: (0, 0)),
            tab_spec(), tab_spec(), tab_spec(),
        ],
        out_specs=[
            pl.BlockSpec((tm, tn), after(_KIND_Q)),
            pl.BlockSpec((tm, KV_COLS), lambda t: (done(t)[0], 0)),
            pl.BlockSpec((tm, tn), after(_KIND_U)),
            pl.BlockSpec((tm, tn), after(_KIND_GB)),
        ],
        out_shape=[
            jax.ShapeDtypeStruct((N_TOKENS, ATTN_WIDTH), BF16),
            jax.ShapeDtypeStruct((N_TOKENS, KV_COLS), BF16),
            jax.ShapeDtypeStruct((N_TOKENS, CONV_WIDTH), BF16),
            jax.ShapeDtypeStruct((N_TOKENS, CONV_WIDTH), BF16),
        ],
        scratch_shapes=[pltpu.VMEM((tm, D_MODEL), BF16)] * 2 + [pltpu.VMEM((tm, tn), F32)] * 3,
        compiler_params=_params(1),
        name="in_proj",
    )(xs, mods, mods, n1g, w_in, qg, kg, cos_t, sa_t, sb_t)


_NQ = SEQ // ATTN_TQ


def _attend(q, chunks):
    def scores(chunk):
        k_ref, _, start, size = chunk
        return lax.dot_general(q, k_ref[start:start + size, :], (((1,), (1,)), ((), ())),
                               preferred_element_type=F32)

    m = acc = None
    s = scores(chunks[0])
    for idx, (_, v_ref, start, size) in enumerate(chunks):
        s_next = scores(chunks[idx + 1]) if idx + 1 < len(chunks) else None
        m_c = jnp.max(s, axis=-1, keepdims=True)
        m_new = m_c if m is None else jnp.maximum(m, m_c)
        p = jnp.exp2(s - m_new).astype(BF16)
        pv = jnp.dot(p, v_ref[start:start + size, :], preferred_element_type=F32)
        acc = pv if m is None else jnp.exp2(m - m_new) * acc + pv
        m, s = m_new, s_next
    return acc[:, :HEAD_DIM] / acc[:, HEAD_DIM:]


def _attend_heads(q_ref, o_ref, chunks):
    tq = q_ref.shape[0]
    q = jnp.concatenate([q_ref[:, r * HEAD_DIM:(r + 1) * HEAD_DIM] for r in range(Q_PER_KV)], axis=0)
    o = _attend(q, chunks).astype(BF16)
    for r in range(Q_PER_KV):
        o_ref[:, r * HEAD_DIM:(r + 1) * HEAD_DIM] = o[r * tq:(r + 1) * tq, :]


def _attn_latent_kernel(q_ref, kl_ref, vl_ref, kc_ref, vc_ref, o_ref):
    chunks = [(kl_ref, vl_ref, c * ATTN_TK, ATTN_TK) for c in range(SEQ // ATTN_TK)]
    _attend_heads(q_ref, o_ref, chunks + [(kc_ref, vc_ref, 0, CTX_LEN)])


def _attn_context_kernel(q_ref, kc_ref, vc_ref, o_ref):
    _attend_heads(q_ref, o_ref, [(kc_ref, vc_ref, 0, CTX_LEN)])


def _attention(q, kv):
    tq = ATTN_TQ
    ctx_blk0 = N_LATENT // CTX_LEN
    v_blk0 = KV_WIDTH // (2 * HEAD_DIM)
    width = Q_PER_KV * HEAD_DIM

    attn = pl.pallas_call(
        _attn_latent_kernel,
        grid=(BATCH, N_KV_HEADS, _NQ),
        in_specs=[
            pl.BlockSpec((tq, width), lambda b, g, qi: (b * _NQ + qi, g)),
            pl.BlockSpec((SEQ, HEAD_DIM), lambda b, g, qi: (b, g)),
            pl.BlockSpec((SEQ, 2 * HEAD_DIM), lambda b, g, qi: (b, v_blk0 + g)),
            pl.BlockSpec((CTX_LEN, HEAD_DIM), lambda b, g, qi: (ctx_blk0 + b, g)),
            pl.BlockSpec((CTX_LEN, 2 * HEAD_DIM), lambda b, g, qi: (ctx_blk0 + b, v_blk0 + g)),
        ],
        out_specs=pl.BlockSpec((tq, width), lambda b, g, qi: (b * _NQ + qi, g)),
        out_shape=jax.ShapeDtypeStruct((N_LATENT, ATTN_WIDTH), BF16),
        compiler_params=_params(3),
        name="attention",
    )(q, kv, kv, kv, kv)
    attn_ctx = pl.pallas_call(
        _attn_context_kernel,
        grid=(BATCH, N_KV_HEADS),
        in_specs=[
            pl.BlockSpec((CTX_LEN, width), lambda b, g: (ctx_blk0 + b, g)),
            pl.BlockSpec((CTX_LEN, HEAD_DIM), lambda b, g: (ctx_blk0 + b, g)),
            pl.BlockSpec((CTX_LEN, 2 * HEAD_DIM), lambda b, g: (ctx_blk0 + b, v_blk0 + g)),
        ],
        out_specs=pl.BlockSpec((CTX_LEN, width), lambda b, g: (b, g)),
        out_shape=jax.ShapeDtypeStruct((N_CTX, ATTN_WIDTH), BF16),
        compiler_params=_params(2),
        name="attention_ctx",
    )(q, kv, kv)
    return attn, attn_ctx


def _merge_kernel(x_ref, attn_lat_ref, attn_ctx_ref, z_ref, zp_ref, zn_ref, gb_ref, cw_ref, cb_ref, ag_ref,
                  cg_ref, w_ref, g1_ref, sh2_ref, sc2_ref, n2g_ref, xo_ref, h2_ref, cat_scr):
    tm = x_ref.shape[0]
    i = pl.program_id(0)
    attn = jnp.where(i * tm >= N_LATENT, attn_ctx_ref[...], attn_lat_ref[...])

    row = lax.broadcasted_iota(jnp.int32, (tm, 1), 0)
    grow = i * tm + row
    seg = jnp.where(grow >= N_LATENT, CTX_LEN, SEQ)
    pos = jnp.bitwise_and(grow, seg - 1)

    z = z_ref[...].astype(F32)
    z_prev = jnp.where(row == 0, zp_ref[HALO - 1:HALO, :].astype(F32), pltpu.roll(z, 1, axis=0))
    z_prev = jnp.where(pos != 0, z_prev, 0.0)
    z_next = jnp.where(row == tm - 1, zn_ref[0:1, :].astype(F32), pltpu.roll(z, tm - 1, axis=0))
    z_next = jnp.where(pos != seg - 1, z_next, 0.0)
    conv = z_prev * cw_ref[0:1, :] + z * cw_ref[1:2, :] + z_next * cw_ref[2:3, :] + cb_ref[...]
    conv = gb_ref[...].astype(F32) * conv

    cat_scr[:, :ATTN_WIDTH] = _rms(attn.astype(F32), ag_ref[...]).astype(BF16)
    cat_scr[:, ATTN_WIDTH:] = _rms(conv, cg_ref[...]).astype(BF16)
    y = jnp.dot(cat_scr[...], w_ref[...], preferred_element_type=F32)
    x_new = x_ref[...] + g1_ref[...] * y
    xo_ref[...] = x_new
    h2 = _rms(x_new, n2g_ref[...]) * (1.0 + sc2_ref[...]) + sh2_ref[...]
    h2_ref[...] = h2.astype(BF16)


def _merge(xs, attn_lat, attn_ctx, z, gb, mods, layer, cw, cb, ag, cg, w_out, n2g, n_rows):
    tm = MERGE_TM
    hb = tm // HALO
    last_hb = N_TOKENS // HALO - 1
    lat_tiles = N_LATENT // tm
    assert N_LATENT % tm == 0 and N_CTX % tm == 0

    def mod_spec(chunk):
        return pl.BlockSpec((None, None, 1, D_MODEL), lambda i: (layer, _mod_row(i, tm), 0, chunk))

    def vec_spec(rows, width):
        return pl.BlockSpec((rows, width), lambda i: (0, 0))

    return pl.pallas_call(
        _merge_kernel,
        grid=(n_rows // tm,),
        in_specs=[
            pl.BlockSpec((tm, D_MODEL), lambda i: (i, 0)),
            pl.BlockSpec((tm, ATTN_WIDTH), lambda i: (jnp.minimum(i, lat_tiles - 1), 0)),
            pl.BlockSpec((tm, ATTN_WIDTH), lambda i: (jnp.maximum(i - lat_tiles, 0), 0)),
            pl.BlockSpec((tm, CONV_WIDTH), lambda i: (i, 0)),
            pl.BlockSpec((HALO, CONV_WIDTH), lambda i: (jnp.maximum(i * hb - 1, 0), 0)),
            pl.BlockSpec((HALO, CONV_WIDTH), lambda i: (jnp.minimum((i + 1) * hb, last_hb), 0)),
            pl.BlockSpec((tm, CONV_WIDTH), lambda i: (i, 0)),
            vec_spec(3, CONV_WIDTH), vec_spec(1, CONV_WIDTH),
            vec_spec(1, ATTN_WIDTH), vec_spec(1, CONV_WIDTH),
            pl.BlockSpec((D_MODEL, D_MODEL), lambda i: (0, 0)),
            mod_spec(2), mod_spec(3), mod_spec(4),
            vec_spec(1, D_MODEL),
        ],
        out_specs=[
            pl.BlockSpec((tm, D_MODEL), lambda i: (i, 0)),
            pl.BlockSpec((tm, D_MODEL), lambda i: (i, 0)),
        ],
        out_shape=[
            jax.ShapeDtypeStruct((n_rows, D_MODEL), F32),
            jax.ShapeDtypeStruct((n_rows, D_MODEL), BF16),
        ],
        scratch_shapes=[pltpu.VMEM((tm, D_MODEL), BF16)],
        compiler_params=_params(1),
        name="merge_out_proj",
    )(xs, attn_lat, attn_ctx, z, z, z, gb, cw, cb, ag, cg, w_out, mods, mods, mods, n2g)


def _mlp_kernel(h_ref, w1_ref, w2_ref, x_hbm, g2_ref, o_ref, x_buf, x_sem):
    i, k = pl.program_id(0), pl.program_id(1)
    tm = o_ref.shape[0]
    last_k = pl.num_programs(1) - 1

    def x_copy():
        return pltpu.make_async_copy(x_hbm.at[pl.ds(pl.multiple_of(i * tm, tm), tm), :], x_buf, x_sem)

    @pl.when(k == 0)
    def _():
        x_copy().start()
        o_ref[...] = jnp.zeros_like(o_ref)

    def up(s):
        return jnp.dot(h_ref[...], w1_ref[:, s * MLP_SUB:(s + 1) * MLP_SUB], preferred_element_type=F32)

    n_sub = w1_ref.shape[1] // MLP_SUB
    acts = []
    a_next = up(0)
    for s in range(n_sub):
        a = a_next
        if s + 1 < n_sub:
            a_next = up(s + 1)
        a = jnp.maximum(a, 0.0)
        acts.append((a * a).astype(BF16))
    p = jnp.concatenate(acts, axis=1)
    for n in range(D_MODEL // MLP_OUT_SLAB):
        cols = slice(n * MLP_OUT_SLAB, (n + 1) * MLP_OUT_SLAB)
        o_ref[:, cols] += jnp.dot(p, w2_ref[:, cols], preferred_element_type=F32)

    @pl.when(k == last_k)
    def _():
        x_copy().wait()
        o_ref[...] = x_buf[...] + g2_ref[...] * o_ref[...]


def _mlp(h2, x_new, mods, layer, w1, w2, n_rows):
    tm, th = MLP_TM, MLP_TH
    return pl.pallas_call(
        _mlp_kernel,
        grid=(n_rows // tm, MLP_HIDDEN // th),
        in_specs=[
            pl.BlockSpec((tm, D_MODEL), lambda i, k: (i, 0)),
            pl.BlockSpec((D_MODEL, th), lambda i, k: (0, k)),
            pl.BlockSpec((th, D_MODEL), lambda i, k: (k, 0)),
            pl.BlockSpec(memory_space=pl.ANY),
            pl.BlockSpec((None, None, 1, D_MODEL), lambda i, k: (layer, _mod_row(i, tm), 0, 5)),
        ],
        out_specs=pl.BlockSpec((tm, D_MODEL), lambda i, k: (i, 0)),
        out_shape=jax.ShapeDtypeStruct((n_rows, D_MODEL), F32),
        scratch_shapes=[pltpu.VMEM((tm, D_MODEL), F32), pltpu.SemaphoreType.DMA(())],
        compiler_params=_params(2),
        name="mlp",
    )(h2, w1, w2, x_new, mods)


def _rope_tables(tm):
    t = jnp.arange(SEQ)
    row = (t // GRID_W).astype(F32)
    col = (t % GRID_W).astype(F32)
    inv_freq = ROPE_THETA ** (-jnp.arange(0, AXIS_DIM, 2, dtype=F32) / AXIS_DIM)
    ang_r = row[:, None] * inv_freq[None, :]
    ang_c = col[:, None] * inv_freq[None, :]
    ang = jnp.concatenate([ang_r, ang_r, ang_c, ang_c], axis=-1)
    cos, sin = jnp.cos(ang), jnp.sin(ang)
    first = (jnp.arange(HEAD_DIM) % AXIS_DIM) < AXIS_DIM // 2
    sin_a = jnp.where(first, -sin, 0.0)
    sin_b = jnp.where(first, 0.0, sin)
    ident = jnp.zeros((tm, HEAD_DIM), F32)
    return (jnp.concatenate([cos, ident + 1.0]), jnp.concatenate([sin_a, ident]),
            jnp.concatenate([sin_b, ident]))


def kernel(x, c, ctx, c_ctx, w_ada, b_ada, norm1_g, w_in, q_norm_g, k_norm_g, conv_w, conv_b,
           attn_out_g, conv_out_g, w_out, norm2_g, w_mlp_in, w_mlp_out):
    assert x.shape == (BATCH, SEQ, D_MODEL) and ctx.shape == (BATCH, CTX_LEN, D_MODEL)
    xs = jnp.concatenate([x.reshape(N_LATENT, D_MODEL), ctx.reshape(N_CTX, D_MODEL)], axis=0)
    cond = jnp.concatenate([c, c_ctx[None, :], jnp.zeros((MOD_ROWS - BATCH - 1, D_MODEL), F32)], axis=0)
    mods = _adaln_all(cond, w_ada, b_ada).reshape(DEPTH, MOD_ROWS, 1, N_MOD * D_MODEL)
    cos_t, sa_t, sb_t = _rope_tables(INPROJ_TM)

    for l in range(DEPTH):
        q, kv, z, gb = _in_proj(xs, mods, l, norm1_g[l][None], w_in[l].astype(BF16),
                                q_norm_g[l][None], k_norm_g[l][None], cos_t, sa_t, sb_t)
        attn_lat, attn_ctx = _attention(q, kv)
        n_rows = N_TOKENS if l < DEPTH - 1 else N_LATENT
        x_new, h2 = _merge(xs, attn_lat, attn_ctx, z, gb, mods, l, conv_w[l], conv_b[l][None], attn_out_g[l][None],
                           conv_out_g[l][None], w_out[l].astype(BF16), norm2_g[l][None], n_rows)
        xs = _mlp(h2, x_new, mods, l, w_mlp_in[l].astype(BF16), w_mlp_out[l].astype(BF16), n_rows)
    return xs.reshape(BATCH, SEQ, D_MODEL)
```

```python
import jax
import jax.numpy as jnp
from jax import lax
from jax.experimental import pallas as pl
from jax.experimental.pallas import tpu as pltpu

D_MODEL = 2048
BATCH = 4
SEQ = 4096
DEPTH = 4
CTX_LEN = 256
GRID_W = 64
HEAD_DIM = 128
N_HEADS = 8
N_KV_HEADS = 2
Q_PER_KV = N_HEADS // N_KV_HEADS
ATTN_WIDTH = N_HEADS * HEAD_DIM
KV_WIDTH = N_KV_HEADS * HEAD_DIM
CONV_WIDTH = D_MODEL - ATTN_WIDTH
IN_WIDTH = ATTN_WIDTH + 2 * KV_WIDTH + 3 * CONV_WIDTH
MLP_HIDDEN = 4 * D_MODEL
N_MOD = 6
ROPE_THETA = 10000.0
AXIS_DIM = HEAD_DIM // 2
EPS = 1e-6
LOG2_E = 1.4426950408889634
KV_COLS = KV_WIDTH + 2 * KV_WIDTH

N_LATENT = BATCH * SEQ
N_CTX = BATCH * CTX_LEN
N_TOKENS = N_LATENT + N_CTX
MOD_ROWS = 8

F32 = jnp.float32
BF16 = jnp.bfloat16

ADA_TN = 1024
INPROJ_TM = 1024
INPROJ_TN = 512
ATTN_TQ = 512
ATTN_TK = 512
MERGE_TM = 256
MLP_TM = 1024
MLP_TH = 1024
MLP_SUB = 512
MLP_OUT_SLAB = 512
HALO = 16
VMEM_LIMIT = 56 * 1024 * 1024


def _params(n_axes, vmem=VMEM_LIMIT):
    return pltpu.CompilerParams(dimension_semantics=("arbitrary",) * n_axes, vmem_limit_bytes=vmem)


def _mod_row(tile, tm):
    return jnp.minimum(tile // (SEQ // tm), BATCH)


def _rms(x, g):
    return x * lax.rsqrt(jnp.mean(x * x, axis=-1, keepdims=True) + EPS) * g


def _ada_kernel(cond_ref, w_ref, b_ref, o_ref):
    cnd = cond_ref[...]
    s = (cnd * jax.nn.sigmoid(cnd)).astype(BF16)
    w = w_ref[...].astype(BF16)
    o_ref[...] = jnp.dot(s, w, preferred_element_type=F32) + b_ref[...]


def _adaln_all(cond, w_ada, b_ada):
    n = N_MOD * D_MODEL
    return pl.pallas_call(
        _ada_kernel,
        grid=(DEPTH, n // ADA_TN),
        in_specs=[
            pl.BlockSpec((MOD_ROWS, D_MODEL), lambda l, j: (0, 0)),
            pl.BlockSpec((None, D_MODEL, ADA_TN), lambda l, j: (l, 0, j)),
            pl.BlockSpec((None, 1, ADA_TN), lambda l, j: (l, 0, j)),
        ],
        out_specs=pl.BlockSpec((None, MOD_ROWS, ADA_TN), lambda l, j: (l, 0, j)),
        out_shape=jax.ShapeDtypeStruct((DEPTH, MOD_ROWS, n), F32),
        compiler_params=_params(2),
        name="adaln",
    )(cond, w_ada, b_ada.reshape(DEPTH, 1, n))


_KIND_Q, _KIND_KV, _KIND_GC, _KIND_U, _KIND_GB = range(5)
_QB = ATTN_WIDTH // INPROJ_TN
_CB = CONV_WIDTH // INPROJ_TN
_GB0 = (ATTN_WIDTH + 2 * KV_WIDTH) // INPROJ_TN
_STEPS = ([(_KIND_Q, b) for b in range(_QB)] + [(_KIND_KV, _QB)]
          + [kb for c in range(_CB) for kb in ((_KIND_GC, _GB0 + _CB + c), (_KIND_U, _GB0 + 2 * _CB + c))]
          + [(_KIND_GB, _GB0 + c) for c in range(_CB)])
_NB = len(_STEPS)
_NORM_ROWS = INPROJ_TM // (_NB - 1)


def _lookup(values, idx):
    out = jnp.int32(values[0])
    for n, v in enumerate(values[1:], start=1):
        out = jnp.where(idx == n, jnp.int32(v), out)
    return out


def _rope(x, cos, sin_a, sin_b):
    return (x * cos + pltpu.roll(x, HEAD_DIM - AXIS_DIM // 2, axis=1) * sin_a
            + pltpu.roll(x, AXIS_DIM // 2, axis=1) * sin_b)


def _inproj_kernel(x_ref, sh_ref, sc_ref, n1g_ref, w_ref, qg_ref, kg_ref, cos_ref, sa_ref, sb_ref,
                   q_ref, kv_ref, z_ref, gb_ref, h_cur, h_next, r0_scr, r1_scr, r2_scr):
    t = pl.program_id(0)
    tm = x_ref.shape[0]
    r_scr = (r0_scr, r1_scr, r2_scr)
    assert _NB % len(r_scr) == 0

    def norm_rows(r0, dst):
        x = x_ref[r0:r0 + _NORM_ROWS, :]
        h = _rms(x, n1g_ref[...]) * (1.0 + sc_ref[...]) + sh_ref[...]
        dst[r0:r0 + _NORM_ROWS, :] = h.astype(BF16)

    def head(r_ref, col0, gain, scale):
        y = _rms(r_ref[:, col0:col0 + HEAD_DIM], gain)
        y = _rope(y, cos_ref[...], sa_ref[...], sb_ref[...])
        return (y * scale).astype(BF16)

    def finish(kind, r_ref, r_before):
        if kind == _KIND_Q:
            for hd in range(INPROJ_TN // HEAD_DIM):
                q_ref[:, hd * HEAD_DIM:(hd + 1) * HEAD_DIM] = head(r_ref, hd * HEAD_DIM, qg_ref[...],
                                                                   HEAD_DIM ** -0.5 * LOG2_E)
        elif kind == _KIND_KV:
            for hd in range(N_KV_HEADS):
                kv_ref[:, hd * HEAD_DIM:(hd + 1) * HEAD_DIM] = head(r_ref, hd * HEAD_DIM, kg_ref[...], 1.0)
                v0 = KV_WIDTH + hd * 2 * HEAD_DIM
                kv_ref[:, v0:v0 + HEAD_DIM] = r_ref[:, KV_WIDTH + hd * HEAD_DIM:
                                                    KV_WIDTH + (hd + 1) * HEAD_DIM].astype(BF16)
                kv_ref[:, v0 + HEAD_DIM:v0 + 2 * HEAD_DIM] = jnp.ones((tm, HEAD_DIM), BF16)
        elif kind == _KIND_U:
            z_ref[...] = (r_before[...] * r_ref[...]).astype(BF16)
        elif kind == _KIND_GB:
            gb_ref[...] = r_ref[...].astype(BF16)

    @pl.when(t == 0)
    def _():
        for c in range(tm // _NORM_ROWS):
            norm_rows(c * _NORM_ROWS, h_cur)
        r_scr[0][...] = jnp.dot(h_cur[...], w_ref[...], preferred_element_type=F32)

    for jp in range(_NB):
        @pl.when(jnp.logical_and(t > 0, (t + _NB - 1) % _NB == jp))
        def _(jp=jp):
            r_scr[(jp + 1) % 3][...] = jnp.dot(h_cur[...], w_ref[...], preferred_element_type=F32)
            if jp < _NB - 1:
                norm_rows(jp * _NORM_ROWS, h_next)
            finish(_STEPS[jp][0], r_scr[jp % 3], r_scr[(jp - 1) % 3])
            if jp == _NB - 2:
                h_cur[...] = h_next[...]


def _in_proj(xs, mods, layer, n1g, w_in, qg, kg, cos_t, sa_t, sb_t):
    tm, tn = INPROJ_TM, INPROJ_TN
    n_tiles = N_TOKENS // tm
    lat_tiles = SEQ // tm
    assert tm % (_NB - 1) == 0 and _NORM_ROWS % 16 == 0

    def norm_tile(t):
        return jnp.where(t == 0, 0, jnp.minimum(t // _NB + 1, n_tiles - 1))

    def done(t):
        tp = jnp.maximum(t - 1, 0)
        return tp // _NB, tp % _NB

    def mod_spec(chunk):
        return pl.BlockSpec((None, None, 1, D_MODEL),
                            lambda t: (layer, _mod_row(norm_tile(t), tm), 0, chunk))

    def tab_spec():
        def index(t):
            i, _ = done(t)
            return (jnp.where(i < N_LATENT // tm, i % lat_tiles, lat_tiles), 0)
        return pl.BlockSpec((tm, HEAD_DIM), index)

    def after(kind):
        firsts = [n for n, (k, _) in enumerate(_STEPS) if k == kind]

        def index(t):
            i, j = done(t)
            col = jnp.int32(0)
            for n in firsts[1:]:
                col = col + (j >= n).astype(jnp.int32)
            return (i, col)
        return index

    return pl.pallas_call(
        _inproj_kernel,
        grid=(n_tiles * _NB + 1,),
        in_specs=[
            pl.BlockSpec((tm, D_MODEL), lambda t: (norm_tile(t), 0)),
            mod_spec(0), mod_spec(1),
            pl.BlockSpec((1, D_MODEL), lambda t: (0, 0)),
            pl.BlockSpec((None, D_MODEL, tn), lambda t: (layer, 0, _lookup([b for _, b in _STEPS], t % _NB))),
            pl.BlockSpec((1, HEAD_DIM), lambda t: (0, 0)),
            pl.BlockSpec((1, HEAD_DIM), lambda t: (0, 0)),
            tab_spec(), tab_spec(), tab_spec(),
        ],
        out_specs=[
            pl.BlockSpec((tm, tn), after(_KIND_Q)),
            pl.BlockSpec((tm, KV_COLS), lambda t: (done(t)[0], 0)),
            pl.BlockSpec((tm, tn), after(_KIND_U)),
            pl.BlockSpec((tm, tn), after(_KIND_GB)),
        ],
        out_shape=[
            jax.ShapeDtypeStruct((N_TOKENS, ATTN_WIDTH), BF16),
            jax.ShapeDtypeStruct((N_TOKENS, KV_COLS), BF16),
            jax.ShapeDtypeStruct((N_TOKENS, CONV_WIDTH), BF16),
            jax.ShapeDtypeStruct((N_TOKENS, CONV_WIDTH), BF16),
        ],
        scratch_shapes=[pltpu.VMEM((tm, D_MODEL), BF16)] * 2 + [pltpu.VMEM((tm, tn), F32)] * 3,
        compiler_params=_params(1),
        name="in_proj",
    )(xs, mods, mods, n1g, w_in, qg, kg, cos_t, sa_t, sb_t)


_NQ = SEQ // ATTN_TQ


def _attend(q, chunks):
    def scores(chunk):
        k_ref, _, start, size = chunk
        return lax.dot_general(q, k_ref[start:start + size, :], (((1,), (1,)), ((), ())),
                               preferred_element_type=F32)

    m = acc = None
    s = scores(chunks[0])
    for idx, (_, v_ref, start, size) in enumerate(chunks):
        s_next = scores(chunks[idx + 1]) if idx + 1 < len(chunks) else None
        m_c = jnp.max(s, axis=-1, keepdims=True)
        m_new = m_c if m is None else jnp.maximum(m, m_c)
        p = jnp.exp2(s - m_new).astype(BF16)
        pv = jnp.dot(p, v_ref[start:start + size, :], preferred_element_type=F32)
        acc = pv if m is None else jnp.exp2(m - m_new) * acc + pv
        m, s = m_new, s_next
    return acc[:, :HEAD_DIM] / acc[:, HEAD_DIM:]


def _attend_heads(q_ref, o_ref, chunks):
    tq = q_ref.shape[0]
    q = jnp.concatenate([q_ref[:, r * HEAD_DIM:(r + 1) * HEAD_DIM] for r in range(Q_PER_KV)], axis=0)
    o = _attend(q, chunks).astype(BF16)
    for r in range(Q_PER_KV):
        o_ref[:, r * HEAD_DIM:(r + 1) * HEAD_DIM] = o[r * tq:(r + 1) * tq, :]


def _attn_latent_kernel(q_ref, kl_ref, vl_ref, kc_ref, vc_ref, o_ref):
    chunks = [(kl_ref, vl_ref, c * ATTN_TK, ATTN_TK) for c in range(SEQ // ATTN_TK)]
    _attend_heads(q_ref, o_ref, chunks + [(kc_ref, vc_ref, 0, CTX_LEN)])


def _attn_context_kernel(q_ref, kc_ref, vc_ref, o_ref):
    _attend_heads(q_ref, o_ref, [(kc_ref, vc_ref, 0, CTX_LEN)])


def _attention(q, kv):
    tq = ATTN_TQ
    ctx_blk0 = N_LATENT // CTX_LEN
    v_blk0 = KV_WIDTH // (2 * HEAD_DIM)
    width = Q_PER_KV * HEAD_DIM

    attn = pl.pallas_call(
        _attn_latent_kernel,
        grid=(BATCH, N_KV_HEADS, _NQ),
        in_specs=[
            pl.BlockSpec((tq, width), lambda b, g, qi: (b * _NQ + qi, g)),
            pl.BlockSpec((SEQ, HEAD_DIM), lambda b, g, qi: (b, g)),
            pl.BlockSpec((SEQ, 2 * HEAD_DIM), lambda b, g, qi: (b, v_blk0 + g)),
            pl.BlockSpec((CTX_LEN, HEAD_DIM), lambda b, g, qi: (ctx_blk0 + b, g)),
            pl.BlockSpec((CTX_LEN, 2 * HEAD_DIM), lambda b, g, qi: (ctx_blk0 + b, v_blk0 + g)),
        ],
        out_specs=pl.BlockSpec((tq, width), lambda b, g, qi: (b * _NQ + qi, g)),
        out_shape=jax.ShapeDtypeStruct((N_LATENT, ATTN_WIDTH), BF16),
        compiler_params=_params(3),
        name="attention",
    )(q, kv, kv, kv, kv)
    attn_ctx = pl.pallas_call(
        _attn_context_kernel,
        grid=(BATCH, N_KV_HEADS),
        in_specs=[
            pl.BlockSpec((CTX_LEN, width), lambda b, g: (ctx_blk0 + b, g)),
            pl.BlockSpec((CTX_LEN, HEAD_DIM), lambda b, g: (ctx_blk0 + b, g)),
            pl.BlockSpec((CTX_LEN, 2 * HEAD_DIM), lambda b, g: (ctx_blk0 + b, v_blk0 + g)),
        ],
        out_specs=pl.BlockSpec((CTX_LEN, width), lambda b, g: (b, g)),
        out_shape=jax.ShapeDtypeStruct((N_CTX, ATTN_WIDTH), BF16),
        compiler_params=_params(2),
        name="attention_ctx",
    )(q, kv, kv)
    return attn, attn_ctx


def _merge_kernel(x_ref, attn_lat_ref, attn_ctx_ref, z_ref, zp_ref, zn_ref, gb_ref, cw_ref, cb_ref, ag_ref,
                  cg_ref, w_ref, g1_ref, sh2_ref, sc2_ref, n2g_ref, xo_ref, h2_ref, cat_scr):
    tm = x_ref.shape[0]
    i = pl.program_id(0)
    attn = jnp.where(i * tm >= N_LATENT, attn_ctx_ref[...], attn_lat_ref[...])

    row = lax.broadcasted_iota(jnp.int32, (tm, 1), 0)
    grow = i * tm + row
    seg = jnp.where(grow >= N_LATENT, CTX_LEN, SEQ)
    pos = jnp.bitwise_and(grow, seg - 1)

    z = z_ref[...].astype(F32)
    z_prev = jnp.where(row == 0, zp_ref[HALO - 1:HALO, :].astype(F32), pltpu.roll(z, 1, axis=0))
    z_prev = jnp.where(pos != 0, z_prev, 0.0)
    z_next = jnp.where(row == tm - 1, zn_ref[0:1, :].astype(F32), pltpu.roll(z, tm - 1, axis=0))
    z_next = jnp.where(pos != seg - 1, z_next, 0.0)
    conv = z_prev * cw_ref[0:1, :] + z * cw_ref[1:2, :] + z_next * cw_ref[2:3, :] + cb_ref[...]
    conv = gb_ref[...].astype(F32) * conv

    cat_scr[:, :ATTN_WIDTH] = _rms(attn.astype(F32), ag_ref[...]).astype(BF16)
    cat_scr[:, ATTN_WIDTH:] = _rms(conv, cg_ref[...]).astype(BF16)
    y = jnp.dot(cat_scr[...], w_ref[...], preferred_element_type=F32)
    x_new = x_ref[...] + g1_ref[...] * y
    xo_ref[...] = x_new
    h2 = _rms(x_new, n2g_ref[...]) * (1.0 + sc2_ref[...]) + sh2_ref[...]
    h2_ref[...] = h2.astype(BF16)


def _merge(xs, attn_lat, attn_ctx, z, gb, mods, layer, cw, cb, ag, cg, w_out, n2g, n_rows):
    tm = MERGE_TM
    hb = tm // HALO
    last_hb = N_TOKENS // HALO - 1
    lat_tiles = N_LATENT // tm
    assert N_LATENT % tm == 0 and N_CTX % tm == 0

    def mod_spec(chunk):
        return pl.BlockSpec((None, None, 1, D_MODEL), lambda i: (layer, _mod_row(i, tm), 0, chunk))

    def vec_spec(rows, width):
        return pl.BlockSpec((rows, width), lambda i: (0, 0))

    return pl.pallas_call(
        _merge_kernel,
        grid=(n_rows // tm,),
        in_specs=[
            pl.BlockSpec((tm, D_MODEL), lambda i: (i, 0)),
            pl.BlockSpec((tm, ATTN_WIDTH), lambda i: (jnp.minimum(i, lat_tiles - 1), 0)),
            pl.BlockSpec((tm, ATTN_WIDTH), lambda i: (jnp.maximum(i - lat_tiles, 0), 0)),
            pl.BlockSpec((tm, CONV_WIDTH), lambda i: (i, 0)),
            pl.BlockSpec((HALO, CONV_WIDTH), lambda i: (jnp.maximum(i * hb - 1, 0), 0)),
            pl.BlockSpec((HALO, CONV_WIDTH), lambda i: (jnp.minimum((i + 1) * hb, last_hb), 0)),
            pl.BlockSpec((tm, CONV_WIDTH), lambda i: (i, 0)),
            vec_spec(3, CONV_WIDTH), vec_spec(1, CONV_WIDTH),
            vec_spec(1, ATTN_WIDTH), vec_spec(1, CONV_WIDTH),
            pl.BlockSpec((None, D_MODEL, D_MODEL), lambda i: (layer, 0, 0)),
            mod_spec(2), mod_spec(3), mod_spec(4),
            vec_spec(1, D_MODEL),
        ],
        out_specs=[
            pl.BlockSpec((tm, D_MODEL), lambda i: (i, 0)),
            pl.BlockSpec((tm, D_MODEL), lambda i: (i, 0)),
        ],
        out_shape=[
            jax.ShapeDtypeStruct((n_rows, D_MODEL), F32),
            jax.ShapeDtypeStruct((n_rows, D_MODEL), BF16),
        ],
        scratch_shapes=[pltpu.VMEM((tm, D_MODEL), BF16)],
        compiler_params=_params(1),
        name="merge_out_proj",
    )(xs, attn_lat, attn_ctx, z, z, z, gb, cw, cb, ag, cg, w_out, mods, mods, mods, n2g)


def _mlp_kernel(h_ref, w1_ref, w2_ref, x_hbm, g2_ref, o_ref, x_buf, x_sem):
    i, k = pl.program_id(0), pl.program_id(1)
    tm = o_ref.shape[0]
    last_k = pl.num_programs(1) - 1

    def x_copy():
        return pltpu.make_async_copy(x_hbm.at[pl.ds(pl.multiple_of(i * tm, tm), tm), :], x_buf, x_sem)

    @pl.when(k == 0)
    def _():
        x_copy().start()
        o_ref[...] = jnp.zeros_like(o_ref)

    def up(s):
        return jnp.dot(h_ref[...], w1_ref[:, s * MLP_SUB:(s + 1) * MLP_SUB], preferred_element_type=F32)

    n_sub = w1_ref.shape[1] // MLP_SUB
    acts = []
    a_next = up(0)
    for s in range(n_sub):
        a = a_next
        if s + 1 < n_sub:
            a_next = up(s + 1)
        a = jnp.maximum(a, 0.0)
        acts.append((a * a).astype(BF16))
    p = jnp.concatenate(acts, axis=1)
    for n in range(D_MODEL // MLP_OUT_SLAB):
        cols = slice(n * MLP_OUT_SLAB, (n + 1) * MLP_OUT_SLAB)
        o_ref[:, cols] += jnp.dot(p, w2_ref[:, cols], preferred_element_type=F32)

    @pl.when(k == last_k)
    def _():
        x_copy().wait()
        o_ref[...] = x_buf[...] + g2_ref[...] * o_ref[...]


def _mlp(h2, x_new, mods, layer, w1, w2, n_rows):
    tm, th = MLP_TM, MLP_TH
    return pl.pallas_call(
        _mlp_kernel,
        grid=(n_rows // tm, MLP_HIDDEN // th),
        in_specs=[
            pl.BlockSpec((tm, D_MODEL), lambda i, k: (i, 0)),
            pl.BlockSpec((None, D_MODEL, th), lambda i, k: (layer, 0, k)),
            pl.BlockSpec((None, th, D_MODEL), lambda i, k: (layer, k, 0)),
            pl.BlockSpec(memory_space=pl.ANY),
            pl.BlockSpec((None, None, 1, D_MODEL), lambda i, k: (layer, _mod_row(i, tm), 0, 5)),
        ],
        out_specs=pl.BlockSpec((tm, D_MODEL), lambda i, k: (i, 0)),
        out_shape=jax.ShapeDtypeStruct((n_rows, D_MODEL), F32),
        scratch_shapes=[pltpu.VMEM((tm, D_MODEL), F32), pltpu.SemaphoreType.DMA(())],
        compiler_params=_params(2),
        name="mlp",
    )(h2, w1, w2, x_new, mods)


def _rope_tables(tm):
    t = jnp.arange(SEQ)
    row = (t // GRID_W).astype(F32)
    col = (t % GRID_W).astype(F32)
    inv_freq = ROPE_THETA ** (-jnp.arange(0, AXIS_DIM, 2, dtype=F32) / AXIS_DIM)
    ang_r = row[:, None] * inv_freq[None, :]
    ang_c = col[:, None] * inv_freq[None, :]
    ang = jnp.concatenate([ang_r, ang_r, ang_c, ang_c], axis=-1)
    cos, sin = jnp.cos(ang), jnp.sin(ang)
    first = (jnp.arange(HEAD_DIM) % AXIS_DIM) < AXIS_DIM // 2
    sin_a = jnp.where(first, -sin, 0.0)
    sin_b = jnp.where(first, 0.0, sin)
    ident = jnp.zeros((tm, HEAD_DIM), F32)
    return (jnp.concatenate([cos, ident + 1.0]), jnp.concatenate([sin_a, ident]),
            jnp.concatenate([sin_b, ident]))


def kernel(x, c, ctx, c_ctx, w_ada, b_ada, norm1_g, w_in, q_norm_g, k_norm_g, conv_w, conv_b,
           attn_out_g, conv_out_g, w_out, norm2_g, w_mlp_in, w_mlp_out):
    assert x.shape == (BATCH, SEQ, D_MODEL) and ctx.shape == (BATCH, CTX_LEN, D_MODEL)
    xs = jnp.concatenate([x.reshape(N_LATENT, D_MODEL), ctx.reshape(N_CTX, D_MODEL)], axis=0)
    cond = jnp.concatenate([c, c_ctx[None, :], jnp.zeros((MOD_ROWS - BATCH - 1, D_MODEL), F32)], axis=0)
    mods = _adaln_all(cond, w_ada, b_ada).reshape(DEPTH, MOD_ROWS, 1, N_MOD * D_MODEL)
    cos_t, sa_t, sb_t = _rope_tables(INPROJ_TM)
    w_in, w_out, w_mlp_in, w_mlp_out = (w.astype(BF16) for w in (w_in, w_out, w_mlp_in, w_mlp_out))

    for l in range(DEPTH):
        q, kv, z, gb = _in_proj(xs, mods, l, norm1_g[l][None], w_in,
                                q_norm_g[l][None], k_norm_g[l][None], cos_t, sa_t, sb_t)
        attn_lat, attn_ctx = _attention(q, kv)
        n_rows = N_TOKENS if l < DEPTH - 1 else N_LATENT
        x_new, h2 = _merge(xs, attn_lat, attn_ctx, z, gb, mods, l, conv_w[l], conv_b[l][None], attn_out_g[l][None],
                           conv_out_g[l][None], w_out, norm2_g[l][None], n_rows)
        xs = _mlp(h2, x_new, mods, l, w_mlp_in, w_mlp_out, n_rows)
    return xs.reshape(BATCH, SEQ, D_MODEL)
```

```python
import jax
import jax.numpy as jnp
from jax import lax
from jax.experimental import pallas as pl
from jax.experimental.pallas import tpu as pltpu

D_MODEL = 2048
BATCH = 4
SEQ = 4096
DEPTH = 4
CTX_LEN = 256
GRID_W = 64
HEAD_DIM = 128
N_HEADS = 8
N_KV_HEADS = 2
Q_PER_KV = N_HEADS // N_KV_HEADS
ATTN_WIDTH = N_HEADS * HEAD_DIM
KV_WIDTH = N_KV_HEADS * HEAD_DIM
CONV_WIDTH = D_MODEL - ATTN_WIDTH
IN_WIDTH = ATTN_WIDTH + 2 * KV_WIDTH + 3 * CONV_WIDTH
MLP_HIDDEN = 4 * D_MODEL
N_MOD = 6
ROPE_THETA = 10000.0
AXIS_DIM = HEAD_DIM // 2
EPS = 1e-6
LOG2_E = 1.4426950408889634
KV_COLS = KV_WIDTH + 2 * KV_WIDTH

N_LATENT = BATCH * SEQ
N_CTX = BATCH * CTX_LEN
N_TOKENS = N_LATENT + N_CTX
MOD_ROWS = 8

F32 = jnp.float32
BF16 = jnp.bfloat16

ADA_TN = 1024
INPROJ_TM = 1024
INPROJ_TN = 512
ATTN_TQ = 512
ATTN_TK = 512
MERGE_TM = 256
MLP_TM = 1024
MLP_TH = 1024
MLP_SUB = 512
MLP_OUT_SLAB = 512
HALO = 16
VMEM_LIMIT = 56 * 1024 * 1024


def _params(n_axes, vmem=VMEM_LIMIT):
    return pltpu.CompilerParams(dimension_semantics=("arbitrary",) * n_axes, vmem_limit_bytes=vmem)


def _mod_row(tile, tm):
    return jnp.minimum(tile // (SEQ // tm), BATCH)


def _rms(x, g):
    return x * lax.rsqrt(jnp.mean(x * x, axis=-1, keepdims=True) + EPS) * g


def _ada_kernel(cond_ref, w_ref, b_ref, o_ref):
    cnd = cond_ref[...]
    s = (cnd * jax.nn.sigmoid(cnd)).astype(BF16)
    w = w_ref[...].astype(BF16)
    o_ref[...] = jnp.dot(s, w, preferred_element_type=F32) + b_ref[...]


def _adaln_all(cond, w_ada, b_ada):
    n = N_MOD * D_MODEL
    return pl.pallas_call(
        _ada_kernel,
        grid=(DEPTH, n // ADA_TN),
        in_specs=[
            pl.BlockSpec((MOD_ROWS, D_MODEL), lambda l, j: (0, 0)),
            pl.BlockSpec((None, D_MODEL, ADA_TN), lambda l, j: (l, 0, j)),
            pl.BlockSpec((None, 1, ADA_TN), lambda l, j: (l, 0, j)),
        ],
        out_specs=pl.BlockSpec((None, MOD_ROWS, ADA_TN), lambda l, j: (l, 0, j)),
        out_shape=jax.ShapeDtypeStruct((DEPTH, MOD_ROWS, n), F32),
        compiler_params=_params(2),
        name="adaln",
    )(cond, w_ada, b_ada.reshape(DEPTH, 1, n))


_KVB = ATTN_WIDTH // INPROJ_TN
_GBB = (ATTN_WIDTH + 2 * KV_WIDTH) // INPROJ_TN
_GCB = _GBB + CONV_WIDTH // INPROJ_TN
_UB = _GCB + CONV_WIDTH // INPROJ_TN
assert _KVB == 2 and _GCB - _GBB == 2 and 2 * KV_WIDTH == INPROJ_TN
_PAIRS = ((0, 1), (_KVB, _GCB), (_UB, _GCB + 1), (_UB + 1, _GBB), (_GBB + 1, None))
_NP = len(_PAIRS)
_SLOT = tuple(j % 3 for j in range(_NP))
_NORM_ROWS = INPROJ_TM // (_NP - 1)


def _lookup(values, idx):
    out = jnp.int32(values[0])
    for n, v in enumerate(values[1:], start=1):
        out = jnp.where(idx == n, jnp.int32(v), out)
    return out


def _rope(x, cos, sin_a, sin_b):
    return (x * cos + pltpu.roll(x, HEAD_DIM - AXIS_DIM // 2, axis=1) * sin_a
            + pltpu.roll(x, AXIS_DIM // 2, axis=1) * sin_b)


def _inproj_kernel(x_ref, sh_ref, sc_ref, n1g_ref, wa_ref, wb_ref, qg_ref, kg_ref, cos_ref, sa_ref, sb_ref,
                   q_ref, kv_ref, z_ref, gb_ref, h_cur, h_next, ra0, ra1, ra2, rb0, rb1, rb2):
    t = pl.program_id(0)
    j = t % _NP
    tm = q_ref.shape[0]
    ra, rb = (ra0, ra1, ra2), (rb0, rb1, rb2)

    def normalised():
        h = _rms(x_ref[...], n1g_ref[...]) * (1.0 + sc_ref[...]) + sh_ref[...]
        return h.astype(BF16)

    def head(r_ref, col0, gain, scale):
        y = _rms(r_ref[:, col0:col0 + HEAD_DIM], gain)
        y = _rope(y, cos_ref[...], sa_ref[...], sb_ref[...])
        return (y * scale).astype(BF16)

    def multiply(step):
        slot = _SLOT[step]
        ra[slot][...] = jnp.dot(h_cur[...], wa_ref[...], preferred_element_type=F32)
        if _PAIRS[step][1] is not None:
            rb[slot][...] = jnp.dot(h_cur[...], wb_ref[...], preferred_element_type=F32)

    def finish(step):
        if step == 1:
            for half, r_ref in enumerate((ra[_SLOT[0]], rb[_SLOT[0]])):
                for hd in range(INPROJ_TN // HEAD_DIM):
                    c0 = half * INPROJ_TN + hd * HEAD_DIM
                    q_ref[:, c0:c0 + HEAD_DIM] = head(r_ref, hd * HEAD_DIM, qg_ref[...],
                                                      HEAD_DIM ** -0.5 * LOG2_E)
        elif step == 2:
            r_ref = ra[_SLOT[1]]
            for hd in range(N_KV_HEADS):
                kv_ref[:, hd * HEAD_DIM:(hd + 1) * HEAD_DIM] = head(r_ref, hd * HEAD_DIM, kg_ref[...], 1.0)
                v0 = KV_WIDTH + hd * 2 * HEAD_DIM
                kv_ref[:, v0:v0 + HEAD_DIM] = r_ref[:, KV_WIDTH + hd * HEAD_DIM:
                                                    KV_WIDTH + (hd + 1) * HEAD_DIM].astype(BF16)
                kv_ref[:, v0 + HEAD_DIM:v0 + 2 * HEAD_DIM] = jnp.ones((tm, HEAD_DIM), BF16)
        elif step == 3:
            z_ref[...] = (rb[_SLOT[1]][...] * ra[_SLOT[2]][...]).astype(BF16)
        elif step == 4:
            z_ref[...] = (rb[_SLOT[2]][...] * ra[_SLOT[3]][...]).astype(BF16)
            gb_ref[...] = rb[_SLOT[3]][...].astype(BF16)
        elif step == 0:
            gb_ref[...] = ra[_SLOT[4]][...].astype(BF16)

    @pl.when(jnp.logical_and(t >= 1, t < _NP))
    def _():
        r0 = pl.multiple_of((t - 1) * _NORM_ROWS, _NORM_ROWS)
        h_next[pl.ds(r0, _NORM_ROWS), :] = normalised()

    @pl.when(t == _NP - 1)
    def _():
        h_cur[...] = h_next[...]

    @pl.when(t == _NP)
    def _():
        multiply(0)

    for step in range(_NP):
        @pl.when(jnp.logical_and(t > _NP, j == step))
        def _(step=step):
            multiply(step)
            if step >= 1:
                h_next[(step - 1) * _NORM_ROWS:step * _NORM_ROWS, :] = normalised()
            finish(step)
            if step == _NP - 1:
                h_cur[...] = h_next[...]


def _in_proj(xs, mods, layer, n1g, w_in, qg, kg, cos_t, sa_t, sb_t):
    tm, tn = INPROJ_TM, INPROJ_TN
    n_tiles = N_TOKENS // tm
    lat_tiles = SEQ // tm
    slices = _NP - 1
    assert tm % slices == 0 and _NORM_ROWS % 16 == 0

    def tile_of(t):
        return t // _NP - 1

    def norm_block(t):
        tile = jnp.clip(tile_of(t) + 1, 0, n_tiles - 1)
        return tile * slices + jnp.clip(t % _NP - 1, 0, slices - 1)

    def fin_tile(t):
        return jnp.clip(tile_of(t) - (t % _NP == 0).astype(jnp.int32), 0, n_tiles - 1)

    def mod_spec(chunk):
        return pl.BlockSpec((None, None, 1, D_MODEL),
                            lambda t: (layer, _mod_row(norm_block(t) // slices, tm), 0, chunk))

    def w_spec(side):
        blocks = [p[side] if p[side] is not None else p[0] for p in _PAIRS]
        return pl.BlockSpec((None, D_MODEL, tn), lambda t: (layer, 0, _lookup(blocks, t % _NP)))

    def tab_spec():
        def index(t):
            i = fin_tile(t)
            return (jnp.where(i < N_LATENT // tm, i % lat_tiles, lat_tiles), 0)
        return pl.BlockSpec((tm, HEAD_DIM), index)

    def second_col(t, first_step):
        j = t % _NP
        return jnp.logical_and(t > _NP, jnp.logical_or(j == 0, j > first_step)).astype(jnp.int32)

    return pl.pallas_call(
        _inproj_kernel,
        grid=((n_tiles + 1) * _NP + 1,),
        in_specs=[
            pl.BlockSpec((_NORM_ROWS, D_MODEL), lambda t: (norm_block(t), 0)),
            mod_spec(0), mod_spec(1),
            pl.BlockSpec((1, D_MODEL), lambda t: (0, 0)),
            w_spec(0), w_spec(1),
            pl.BlockSpec((1, HEAD_DIM), lambda t: (0, 0)),
            pl.BlockSpec((1, HEAD_DIM), lambda t: (0, 0)),
            tab_spec(), tab_spec(), tab_spec(),
        ],
        out_specs=[
            pl.BlockSpec((tm, ATTN_WIDTH), lambda t: (fin_tile(t), 0)),
            pl.BlockSpec((tm, KV_COLS), lambda t: (fin_tile(t), 0)),
            pl.BlockSpec((tm, tn), lambda t: (fin_tile(t), second_col(t, 3))),
            pl.BlockSpec((tm, tn), lambda t: (fin_tile(t), second_col(t, 4))),
        ],
        out_shape=[
            jax.ShapeDtypeStruct((N_TOKENS, ATTN_WIDTH), BF16),
            jax.ShapeDtypeStruct((N_TOKENS, KV_COLS), BF16),
            jax.ShapeDtypeStruct((N_TOKENS, CONV_WIDTH), BF16),
            jax.ShapeDtypeStruct((N_TOKENS, CONV_WIDTH), BF16),
        ],
        scratch_shapes=[pltpu.VMEM((tm, D_MODEL), BF16)] * 2 + [pltpu.VMEM((tm, tn), F32)] * 6,
        compiler_params=_params(1),
        name="in_proj",
    )(xs, mods, mods, n1g, w_in, w_in, qg, kg, cos_t, sa_t, sb_t)


_NQ = SEQ // ATTN_TQ


def _attend(q, chunks):
    def scores(chunk):
        k_ref, _, start, size = chunk
        return lax.dot_general(q, k_ref[start:start + size, :], (((1,), (1,)), ((), ())),
                               preferred_element_type=F32)

    m = acc = None
    s = scores(chunks[0])
    for idx, (_, v_ref, start, size) in enumerate(chunks):
        s_next = scores(chunks[idx + 1]) if idx + 1 < len(chunks) else None
        m_c = jnp.max(s, axis=-1, keepdims=True)
        m_new = m_c if m is None else jnp.maximum(m, m_c)
        p = jnp.exp2(s - m_new).astype(BF16)
        pv = jnp.dot(p, v_ref[start:start + size, :], preferred_element_type=F32)
        acc = pv if m is None else jnp.exp2(m - m_new) * acc + pv
        m, s = m_new, s_next
    return acc[:, :HEAD_DIM] / acc[:, HEAD_DIM:]


def _attend_heads(q_ref, o_ref, chunks):
    tq = q_ref.shape[0]
    q = jnp.concatenate([q_ref[:, r * HEAD_DIM:(r + 1) * HEAD_DIM] for r in range(Q_PER_KV)], axis=0)
    o = _attend(q, chunks).astype(BF16)
    for r in range(Q_PER_KV):
        o_ref[:, r * HEAD_DIM:(r + 1) * HEAD_DIM] = o[r * tq:(r + 1) * tq, :]


def _attn_latent_kernel(q_ref, kl_ref, vl_ref, kc_ref, vc_ref, o_ref):
    chunks = [(kl_ref, vl_ref, c * ATTN_TK, ATTN_TK) for c in range(SEQ // ATTN_TK)]
    _attend_heads(q_ref, o_ref, chunks + [(kc_ref, vc_ref, 0, CTX_LEN)])


def _attn_context_kernel(q_ref, kc_ref, vc_ref, o_ref):
    _attend_heads(q_ref, o_ref, [(kc_ref, vc_ref, 0, CTX_LEN)])


def _attention(q, kv):
    tq = ATTN_TQ
    ctx_blk0 = N_LATENT // CTX_LEN
    v_blk0 = KV_WIDTH // (2 * HEAD_DIM)
    width = Q_PER_KV * HEAD_DIM

    attn = pl.pallas_call(
        _attn_latent_kernel,
        grid=(BATCH, N_KV_HEADS, _NQ),
        in_specs=[
            pl.BlockSpec((tq, width), lambda b, g, qi: (b * _NQ + qi, g)),
            pl.BlockSpec((SEQ, HEAD_DIM), lambda b, g, qi: (b, g)),
            pl.BlockSpec((SEQ, 2 * HEAD_DIM), lambda b, g, qi: (b, v_blk0 + g)),
            pl.BlockSpec((CTX_LEN, HEAD_DIM), lambda b, g, qi: (ctx_blk0 + b, g)),
            pl.BlockSpec((CTX_LEN, 2 * HEAD_DIM), lambda b, g, qi: (ctx_blk0 + b, v_blk0 + g)),
        ],
        out_specs=pl.BlockSpec((tq, width), lambda b, g, qi: (b * _NQ + qi, g)),
        out_shape=jax.ShapeDtypeStruct((N_LATENT, ATTN_WIDTH), BF16),
        compiler_params=_params(3),
        name="attention",
    )(q, kv, kv, kv, kv)
    attn_ctx = pl.pallas_call(
        _attn_context_kernel,
        grid=(BATCH, N_KV_HEADS),
        in_specs=[
            pl.BlockSpec((CTX_LEN, width), lambda b, g: (ctx_blk0 + b, g)),
            pl.BlockSpec((CTX_LEN, HEAD_DIM), lambda b, g: (ctx_blk0 + b, g)),
            pl.BlockSpec((CTX_LEN, 2 * HEAD_DIM), lambda b, g: (ctx_blk0 + b, v_blk0 + g)),
        ],
        out_specs=pl.BlockSpec((CTX_LEN, width), lambda b, g: (b, g)),
        out_shape=jax.ShapeDtypeStruct((N_CTX, ATTN_WIDTH), BF16),
        compiler_params=_params(2),
        name="attention_ctx",
    )(q, kv, kv)
    return attn, attn_ctx


def _merge_kernel(x_ref, attn_lat_ref, attn_ctx_ref, z_ref, zp_ref, zn_ref, gb_ref, cw_ref, cb_ref, ag_ref,
                  cg_ref, w_ref, g1_ref, sh2_ref, sc2_ref, n2g_ref, xo_ref, h2_ref, cat_scr):
    tm = x_ref.shape[0]
    i = pl.program_id(0)
    attn = jnp.where(i * tm >= N_LATENT, attn_ctx_ref[...], attn_lat_ref[...])

    row = lax.broadcasted_iota(jnp.int32, (tm, 1), 0)
    grow = i * tm + row
    seg = jnp.where(grow >= N_LATENT, CTX_LEN, SEQ)
    pos = jnp.bitwise_and(grow, seg - 1)

    z = z_ref[...].astype(F32)
    z_prev = jnp.where(row == 0, zp_ref[HALO - 1:HALO, :].astype(F32), pltpu.roll(z, 1, axis=0))
    z_prev = jnp.where(pos != 0, z_prev, 0.0)
    z_next = jnp.where(row == tm - 1, zn_ref[0:1, :].astype(F32), pltpu.roll(z, tm - 1, axis=0))
    z_next = jnp.where(pos != seg - 1, z_next, 0.0)
    conv = z_prev * cw_ref[0:1, :] + z * cw_ref[1:2, :] + z_next * cw_ref[2:3, :] + cb_ref[...]
    conv = gb_ref[...].astype(F32) * conv

    cat_scr[:, :ATTN_WIDTH] = _rms(attn.astype(F32), ag_ref[...]).astype(BF16)
    cat_scr[:, ATTN_WIDTH:] = _rms(conv, cg_ref[...]).astype(BF16)
    y = jnp.dot(cat_scr[...], w_ref[...], preferred_element_type=F32)
    x_new = x_ref[...] + g1_ref[...] * y
    xo_ref[...] = x_new
    h2 = _rms(x_new, n2g_ref[...]) * (1.0 + sc2_ref[...]) + sh2_ref[...]
    h2_ref[...] = h2.astype(BF16)


def _merge(xs, attn_lat, attn_ctx, z, gb, mods, layer, cw, cb, ag, cg, w_out, n2g, n_rows):
    tm = MERGE_TM
    hb = tm // HALO
    last_hb = N_TOKENS // HALO - 1
    lat_tiles = N_LATENT // tm
    assert N_LATENT % tm == 0 and N_CTX % tm == 0

    def mod_spec(chunk):
        return pl.BlockSpec((None, None, 1, D_MODEL), lambda i: (layer, _mod_row(i, tm), 0, chunk))

    def vec_spec(rows, width):
        return pl.BlockSpec((rows, width), lambda i: (0, 0))

    return pl.pallas_call(
        _merge_kernel,
        grid=(n_rows // tm,),
        in_specs=[
            pl.BlockSpec((tm, D_MODEL), lambda i: (i, 0)),
            pl.BlockSpec((tm, ATTN_WIDTH), lambda i: (jnp.minimum(i, lat_tiles - 1), 0)),
            pl.BlockSpec((tm, ATTN_WIDTH), lambda i: (jnp.maximum(i - lat_tiles, 0), 0)),
            pl.BlockSpec((tm, CONV_WIDTH), lambda i: (i, 0)),
            pl.BlockSpec((HALO, CONV_WIDTH), lambda i: (jnp.maximum(i * hb - 1, 0), 0)),
            pl.BlockSpec((HALO, CONV_WIDTH), lambda i: (jnp.minimum((i + 1) * hb, last_hb), 0)),
            pl.BlockSpec((tm, CONV_WIDTH), lambda i: (i, 0)),
            vec_spec(3, CONV_WIDTH), vec_spec(1, CONV_WIDTH),
            vec_spec(1, ATTN_WIDTH), vec_spec(1, CONV_WIDTH),
            pl.BlockSpec((None, D_MODEL, D_MODEL), lambda i: (layer, 0, 0)),
            mod_spec(2), mod_spec(3), mod_spec(4),
            vec_spec(1, D_MODEL),
        ],
        out_specs=[
            pl.BlockSpec((tm, D_MODEL), lambda i: (i, 0)),
            pl.BlockSpec((tm, D_MODEL), lambda i: (i, 0)),
        ],
        out_shape=[
            jax.ShapeDtypeStruct((n_rows, D_MODEL), F32),
            jax.ShapeDtypeStruct((n_rows, D_MODEL), BF16),
        ],
        scratch_shapes=[pltpu.VMEM((tm, D_MODEL), BF16)],
        compiler_params=_params(1),
        name="merge_out_proj",
    )(xs, attn_lat, attn_ctx, z, z, z, gb, cw, cb, ag, cg, w_out, mods, mods, mods, n2g)


def _mlp_kernel(h_ref, w1_ref, w2_ref, x_hbm, g2_ref, o_ref, x_buf, x_sem):
    i, k = pl.program_id(0), pl.program_id(1)
    tm = o_ref.shape[0]
    last_k = pl.num_programs(1) - 1

    def x_copy():
        return pltpu.make_async_copy(x_hbm.at[pl.ds(pl.multiple_of(i * tm, tm), tm), :], x_buf, x_sem)

    @pl.when(k == 0)
    def _():
        x_copy().start()
        o_ref[...] = jnp.zeros_like(o_ref)

    def up(s):
        return jnp.dot(h_ref[...], w1_ref[:, s * MLP_SUB:(s + 1) * MLP_SUB], preferred_element_type=F32)

    n_sub = w1_ref.shape[1] // MLP_SUB
    acts = []
    a_next = up(0)
    for s in range(n_sub):
        a = a_next
        if s + 1 < n_sub:
            a_next = up(s + 1)
        a = jnp.maximum(a, 0.0)
        acts.append((a * a).astype(BF16))
    p = jnp.concatenate(acts, axis=1)
    for n in range(D_MODEL // MLP_OUT_SLAB):
        cols = slice(n * MLP_OUT_SLAB, (n + 1) * MLP_OUT_SLAB)
        o_ref[:, cols] += jnp.dot(p, w2_ref[:, cols], preferred_element_type=F32)

    @pl.when(k == last_k)
    def _():
        x_copy().wait()
        o_ref[...] = x_buf[...] + g2_ref[...] * o_ref[...]


def _mlp(h2, x_new, mods, layer, w1, w2, n_rows):
    tm, th = MLP_TM, MLP_TH
    return pl.pallas_call(
        _mlp_kernel,
        grid=(n_rows // tm, MLP_HIDDEN // th),
        in_specs=[
            pl.BlockSpec((tm, D_MODEL), lambda i, k: (i, 0)),
            pl.BlockSpec((None, D_MODEL, th), lambda i, k: (layer, 0, k)),
            pl.BlockSpec((None, th, D_MODEL), lambda i, k: (layer, k, 0)),
            pl.BlockSpec(memory_space=pl.ANY),
            pl.BlockSpec((None, None, 1, D_MODEL), lambda i, k: (layer, _mod_row(i, tm), 0, 5)),
        ],
        out_specs=pl.BlockSpec((tm, D_MODEL), lambda i, k: (i, 0)),
        out_shape=jax.ShapeDtypeStruct((n_rows, D_MODEL), F32),
        scratch_shapes=[pltpu.VMEM((tm, D_MODEL), F32), pltpu.SemaphoreType.DMA(())],
        compiler_params=_params(2),
        name="mlp",
    )(h2, w1, w2, x_new, mods)


def _rope_tables(tm):
    t = jnp.arange(SEQ)
    row = (t // GRID_W).astype(F32)
    col = (t % GRID_W).astype(F32)
    inv_freq = ROPE_THETA ** (-jnp.arange(0, AXIS_DIM, 2, dtype=F32) / AXIS_DIM)
    ang_r = row[:, None] * inv_freq[None, :]
    ang_c = col[:, None] * inv_freq[None, :]
    ang = jnp.concatenate([ang_r, ang_r, ang_c, ang_c], axis=-1)
    cos, sin = jnp.cos(ang), jnp.sin(ang)
    first = (jnp.arange(HEAD_DIM) % AXIS_DIM) < AXIS_DIM // 2
    sin_a = jnp.where(first, -sin, 0.0)
    sin_b = jnp.where(first, 0.0, sin)
    ident = jnp.zeros((tm, HEAD_DIM), F32)
    return (jnp.concatenate([cos, ident + 1.0]), jnp.concatenate([sin_a, ident]),
            jnp.concatenate([sin_b, ident]))


def kernel(x, c, ctx, c_ctx, w_ada, b_ada, norm1_g, w_in, q_norm_g, k_norm_g, conv_w, conv_b,
           attn_out_g, conv_out_g, w_out, norm2_g, w_mlp_in, w_mlp_out):
    assert x.shape == (BATCH, SEQ, D_MODEL) and ctx.shape == (BATCH, CTX_LEN, D_MODEL)
    xs = jnp.concatenate([x.reshape(N_LATENT, D_MODEL), ctx.reshape(N_CTX, D_MODEL)], axis=0)
    cond = jnp.concatenate([c, c_ctx[None, :], jnp.zeros((MOD_ROWS - BATCH - 1, D_MODEL), F32)], axis=0)
    mods = _adaln_all(cond, w_ada, b_ada).reshape(DEPTH, MOD_ROWS, 1, N_MOD * D_MODEL)
    cos_t, sa_t, sb_t = _rope_tables(INPROJ_TM)
    w_in, w_out, w_mlp_in, w_mlp_out = (w.astype(BF16) for w in (w_in, w_out, w_mlp_in, w_mlp_out))

    for l in range(DEPTH):
        q, kv, z, gb = _in_proj(xs, mods, l, norm1_g[l][None], w_in,
                                q_norm_g[l][None], k_norm_g[l][None], cos_t, sa_t, sb_t)
        attn_lat, attn_ctx = _attention(q, kv)
        n_rows = N_TOKENS if l < DEPTH - 1 else N_LATENT
        x_new, h2 = _merge(xs, attn_lat, attn_ctx, z, gb, mods, l, conv_w[l], conv_b[l][None], attn_out_g[l][None],
                           conv_out_g[l][None], w_out, norm2_g[l][None], n_rows)
        xs = _mlp(h2, x_new, mods, l, w_mlp_in, w_mlp_out, n_rows)
    return xs.reshape(BATCH, SEQ, D_MODEL)
```

```python
import jax
import jax.numpy as jnp
from jax import lax
from jax.experimental import pallas as pl
from jax.experimental.pallas import tpu as pltpu

D_MODEL = 2048
BATCH = 4
SEQ = 4096
DEPTH = 4
CTX_LEN = 256
GRID_W = 64
HEAD_DIM = 128
N_HEADS = 8
N_KV_HEADS = 2
Q_PER_KV = N_HEADS // N_KV_HEADS
ATTN_WIDTH = N_HEADS * HEAD_DIM
KV_WIDTH = N_KV_HEADS * HEAD_DIM
CONV_WIDTH = D_MODEL - ATTN_WIDTH
IN_WIDTH = ATTN_WIDTH + 2 * KV_WIDTH + 3 * CONV_WIDTH
MLP_HIDDEN = 4 * D_MODEL
N_MOD = 6
ROPE_THETA = 10000.0
AXIS_DIM = HEAD_DIM // 2
EPS = 1e-6
LOG2_E = 1.4426950408889634
KV_COLS = KV_WIDTH + 2 * KV_WIDTH

N_LATENT = BATCH * SEQ
N_CTX = BATCH * CTX_LEN
N_TOKENS = N_LATENT + N_CTX
MOD_ROWS = 8

F32 = jnp.float32
BF16 = jnp.bfloat16

ADA_TN = 1024
INPROJ_TM = 1024
INPROJ_TN = 512
ATTN_TQ = 512
ATTN_TK = 512
MERGE_TM = 256
MLP_TM = 1024
MLP_TH = 1024
MLP_SUB = 512
MLP_OUT_SLAB = 512
HALO = 16
VMEM_LIMIT = 56 * 1024 * 1024


def _params(n_axes, vmem=VMEM_LIMIT):
    return pltpu.CompilerParams(dimension_semantics=("arbitrary",) * n_axes, vmem_limit_bytes=vmem)


def _mod_row(tile, tm):
    return jnp.minimum(tile // (SEQ // tm), BATCH)


def _rms(x, g):
    return x * lax.rsqrt(jnp.mean(x * x, axis=-1, keepdims=True) + EPS) * g


def _ada_kernel(cond_ref, w_ref, b_ref, o_ref):
    cnd = cond_ref[...]
    s = (cnd * jax.nn.sigmoid(cnd)).astype(BF16)
    w = w_ref[...].astype(BF16)
    o_ref[...] = jnp.dot(s, w, preferred_element_type=F32) + b_ref[...]


def _adaln_all(cond, w_ada, b_ada):
    n = N_MOD * D_MODEL
    return pl.pallas_call(
        _ada_kernel,
        grid=(DEPTH, n // ADA_TN),
        in_specs=[
            pl.BlockSpec((MOD_ROWS, D_MODEL), lambda l, j: (0, 0)),
            pl.BlockSpec((None, D_MODEL, ADA_TN), lambda l, j: (l, 0, j)),
            pl.BlockSpec((None, 1, ADA_TN), lambda l, j: (l, 0, j)),
        ],
        out_specs=pl.BlockSpec((None, MOD_ROWS, ADA_TN), lambda l, j: (l, 0, j)),
        out_shape=jax.ShapeDtypeStruct((DEPTH, MOD_ROWS, n), F32),
        compiler_params=_params(2),
        name="adaln",
    )(cond, w_ada, b_ada.reshape(DEPTH, 1, n))


_KVB = ATTN_WIDTH // INPROJ_TN
_GBB = (ATTN_WIDTH + 2 * KV_WIDTH) // INPROJ_TN
_GCB = _GBB + CONV_WIDTH // INPROJ_TN
_UB = _GCB + CONV_WIDTH // INPROJ_TN
assert _KVB == 2 and _GCB - _GBB == 2 and 2 * KV_WIDTH == INPROJ_TN
_PAIRS = ((0, 1), (_KVB, _GCB), (_UB, _GCB + 1), (_UB + 1, _GBB), (_GBB + 1, None))
_NP = len(_PAIRS)
_SLOT = tuple(j % 3 for j in range(_NP))
_NORM_ROWS = INPROJ_TM // (_NP - 1)


def _lookup(values, idx):
    out = jnp.int32(values[0])
    for n, v in enumerate(values[1:], start=1):
        out = jnp.where(idx == n, jnp.int32(v), out)
    return out


def _rope(x, cos, sin_a, sin_b):
    return (x * cos + pltpu.roll(x, HEAD_DIM - AXIS_DIM // 2, axis=1) * sin_a
            + pltpu.roll(x, AXIS_DIM // 2, axis=1) * sin_b)


def _inproj_kernel(xl_ref, xc_ref, sh_ref, sc_ref, n1g_ref, wa_ref, wb_ref, qg_ref, kg_ref, cos_ref, sa_ref, sb_ref,
                   q_ref, kv_ref, z_ref, gb_ref, h_cur, h_next, ra0, ra1, ra2, rb0, rb1, rb2):
    t = pl.program_id(0)
    j = t % _NP
    tm = q_ref.shape[0]
    ra, rb = (ra0, ra1, ra2), (rb0, rb1, rb2)

    def normalised():
        norm_tile = jnp.minimum(t // _NP, N_TOKENS // tm - 1)
        x = jnp.where(norm_tile * tm >= N_LATENT, xc_ref[...], xl_ref[...])
        h = _rms(x, n1g_ref[...]) * (1.0 + sc_ref[...]) + sh_ref[...]
        return h.astype(BF16)

    def head(r_ref, col0, gain, scale):
        y = _rms(r_ref[:, col0:col0 + HEAD_DIM], gain)
        y = _rope(y, cos_ref[...], sa_ref[...], sb_ref[...])
        return (y * scale).astype(BF16)

    def multiply(step):
        slot = _SLOT[step]
        ra[slot][...] = jnp.dot(h_cur[...], wa_ref[...], preferred_element_type=F32)
        if _PAIRS[step][1] is not None:
            rb[slot][...] = jnp.dot(h_cur[...], wb_ref[...], preferred_element_type=F32)

    def finish(step):
        if step == 1:
            for half, r_ref in enumerate((ra[_SLOT[0]], rb[_SLOT[0]])):
                for hd in range(INPROJ_TN // HEAD_DIM):
                    c0 = half * INPROJ_TN + hd * HEAD_DIM
                    q_ref[:, c0:c0 + HEAD_DIM] = head(r_ref, hd * HEAD_DIM, qg_ref[...],
                                                      HEAD_DIM ** -0.5 * LOG2_E)
        elif step == 2:
            r_ref = ra[_SLOT[1]]
            for hd in range(N_KV_HEADS):
                kv_ref[:, hd * HEAD_DIM:(hd + 1) * HEAD_DIM] = head(r_ref, hd * HEAD_DIM, kg_ref[...], 1.0)
                v0 = KV_WIDTH + hd * 2 * HEAD_DIM
                kv_ref[:, v0:v0 + HEAD_DIM] = r_ref[:, KV_WIDTH + hd * HEAD_DIM:
                                                    KV_WIDTH + (hd + 1) * HEAD_DIM].astype(BF16)
                kv_ref[:, v0 + HEAD_DIM:v0 + 2 * HEAD_DIM] = jnp.ones((tm, HEAD_DIM), BF16)
        elif step == 3:
            z_ref[...] = (rb[_SLOT[1]][...] * ra[_SLOT[2]][...]).astype(BF16)
        elif step == 4:
            z_ref[...] = (rb[_SLOT[2]][...] * ra[_SLOT[3]][...]).astype(BF16)
            gb_ref[...] = rb[_SLOT[3]][...].astype(BF16)
        elif step == 0:
            gb_ref[...] = ra[_SLOT[4]][...].astype(BF16)

    @pl.when(jnp.logical_and(t >= 1, t < _NP))
    def _():
        r0 = pl.multiple_of((t - 1) * _NORM_ROWS, _NORM_ROWS)
        h_next[pl.ds(r0, _NORM_ROWS), :] = normalised()

    @pl.when(t == _NP - 1)
    def _():
        h_cur[...] = h_next[...]

    @pl.when(t == _NP)
    def _():
        multiply(0)

    for step in range(_NP):
        @pl.when(jnp.logical_and(t > _NP, j == step))
        def _(step=step):
            multiply(step)
            if step >= 1:
                h_next[(step - 1) * _NORM_ROWS:step * _NORM_ROWS, :] = normalised()
            finish(step)
            if step == _NP - 1:
                h_cur[...] = h_next[...]


def _in_proj(stream, mods, layer, n1g, w_in, qg, kg, cos_t, sa_t, sb_t):
    tm, tn = INPROJ_TM, INPROJ_TN
    n_tiles = N_TOKENS // tm
    lat_tiles = SEQ // tm
    slices = _NP - 1
    assert tm % slices == 0 and _NORM_ROWS % 16 == 0
    x_lat, x_ctx, ctx_row0 = stream
    lat_blocks = N_LATENT // _NORM_ROWS
    ctx_blocks = N_CTX // _NORM_ROWS

    def tile_of(t):
        return t // _NP - 1

    def norm_block(t):
        tile = jnp.clip(tile_of(t) + 1, 0, n_tiles - 1)
        return tile * slices + jnp.clip(t % _NP - 1, 0, slices - 1)

    def fin_tile(t):
        return jnp.clip(tile_of(t) - (t % _NP == 0).astype(jnp.int32), 0, n_tiles - 1)

    def mod_spec(chunk):
        return pl.BlockSpec((None, None, 1, D_MODEL),
                            lambda t: (layer, _mod_row(norm_block(t) // slices, tm), 0, chunk))

    def w_spec(side):
        blocks = [p[side] if p[side] is not None else p[0] for p in _PAIRS]
        return pl.BlockSpec((None, D_MODEL, tn), lambda t: (layer, 0, _lookup(blocks, t % _NP)))

    def tab_spec():
        def index(t):
            i = fin_tile(t)
            return (jnp.where(i < N_LATENT // tm, i % lat_tiles, lat_tiles), 0)
        return pl.BlockSpec((tm, HEAD_DIM), index)

    def second_col(t, first_step):
        j = t % _NP
        return jnp.logical_and(t > _NP, jnp.logical_or(j == 0, j > first_step)).astype(jnp.int32)

    return pl.pallas_call(
        _inproj_kernel,
        grid=((n_tiles + 1) * _NP + 1,),
        in_specs=[
            pl.BlockSpec((_NORM_ROWS, D_MODEL), lambda t: (jnp.minimum(norm_block(t), lat_blocks - 1), 0)),
            pl.BlockSpec((_NORM_ROWS, D_MODEL),
                         lambda t: (ctx_row0 // _NORM_ROWS + jnp.clip(norm_block(t) - lat_blocks, 0, ctx_blocks - 1), 0)),
            mod_spec(0), mod_spec(1),
            pl.BlockSpec((1, D_MODEL), lambda t: (0, 0)),
            w_spec(0), w_spec(1),
            pl.BlockSpec((1, HEAD_DIM), lambda t: (0, 0)),
            pl.BlockSpec((1, HEAD_DIM), lambda t: (0, 0)),
            tab_spec(), tab_spec(), tab_spec(),
        ],
        out_specs=[
            pl.BlockSpec((tm, ATTN_WIDTH), lambda t: (fin_tile(t), 0)),
            pl.BlockSpec((tm, KV_COLS), lambda t: (fin_tile(t), 0)),
            pl.BlockSpec((tm, tn), lambda t: (fin_tile(t), second_col(t, 3))),
            pl.BlockSpec((tm, tn), lambda t: (fin_tile(t), second_col(t, 4))),
        ],
        out_shape=[
            jax.ShapeDtypeStruct((N_TOKENS, ATTN_WIDTH), BF16),
            jax.ShapeDtypeStruct((N_TOKENS, KV_COLS), BF16),
            jax.ShapeDtypeStruct((N_TOKENS, CONV_WIDTH), BF16),
            jax.ShapeDtypeStruct((N_TOKENS, CONV_WIDTH), BF16),
        ],
        scratch_shapes=[pltpu.VMEM((tm, D_MODEL), BF16)] * 2 + [pltpu.VMEM((tm, tn), F32)] * 6,
        compiler_params=_params(1),
        name="in_proj",
    )(x_lat, x_ctx, mods, mods, n1g, w_in, w_in, qg, kg, cos_t, sa_t, sb_t)


_NQ = SEQ // ATTN_TQ


def _attend(q, chunks):
    def scores(chunk):
        k_ref, _, start, size = chunk
        return lax.dot_general(q, k_ref[start:start + size, :], (((1,), (1,)), ((), ())),
                               preferred_element_type=F32)

    m = acc = None
    s = scores(chunks[0])
    for idx, (_, v_ref, start, size) in enumerate(chunks):
        s_next = scores(chunks[idx + 1]) if idx + 1 < len(chunks) else None
        m_c = jnp.max(s, axis=-1, keepdims=True)
        m_new = m_c if m is None else jnp.maximum(m, m_c)
        p = jnp.exp2(s - m_new).astype(BF16)
        pv = jnp.dot(p, v_ref[start:start + size, :], preferred_element_type=F32)
        acc = pv if m is None else jnp.exp2(m - m_new) * acc + pv
        m, s = m_new, s_next
    return acc[:, :HEAD_DIM] / acc[:, HEAD_DIM:]


def _attend_heads(q_ref, o_ref, chunks):
    tq = q_ref.shape[0]
    q = jnp.concatenate([q_ref[:, r * HEAD_DIM:(r + 1) * HEAD_DIM] for r in range(Q_PER_KV)], axis=0)
    o = _attend(q, chunks).astype(BF16)
    for r in range(Q_PER_KV):
        o_ref[:, r * HEAD_DIM:(r + 1) * HEAD_DIM] = o[r * tq:(r + 1) * tq, :]


def _attn_latent_kernel(q_ref, kl_ref, vl_ref, kc_ref, vc_ref, o_ref):
    chunks = [(kl_ref, vl_ref, c * ATTN_TK, ATTN_TK) for c in range(SEQ // ATTN_TK)]
    _attend_heads(q_ref, o_ref, chunks + [(kc_ref, vc_ref, 0, CTX_LEN)])


def _attn_context_kernel(q_ref, kc_ref, vc_ref, o_ref):
    _attend_heads(q_ref, o_ref, [(kc_ref, vc_ref, 0, CTX_LEN)])


def _attention(q, kv):
    tq = ATTN_TQ
    ctx_blk0 = N_LATENT // CTX_LEN
    v_blk0 = KV_WIDTH // (2 * HEAD_DIM)
    width = Q_PER_KV * HEAD_DIM

    attn = pl.pallas_call(
        _attn_latent_kernel,
        grid=(BATCH, N_KV_HEADS, _NQ),
        in_specs=[
            pl.BlockSpec((tq, width), lambda b, g, qi: (b * _NQ + qi, g)),
            pl.BlockSpec((SEQ, HEAD_DIM), lambda b, g, qi: (b, g)),
            pl.BlockSpec((SEQ, 2 * HEAD_DIM), lambda b, g, qi: (b, v_blk0 + g)),
            pl.BlockSpec((CTX_LEN, HEAD_DIM), lambda b, g, qi: (ctx_blk0 + b, g)),
            pl.BlockSpec((CTX_LEN, 2 * HEAD_DIM), lambda b, g, qi: (ctx_blk0 + b, v_blk0 + g)),
        ],
        out_specs=pl.BlockSpec((tq, width), lambda b, g, qi: (b * _NQ + qi, g)),
        out_shape=jax.ShapeDtypeStruct((N_LATENT, ATTN_WIDTH), BF16),
        compiler_params=_params(3),
        name="attention",
    )(q, kv, kv, kv, kv)
    attn_ctx = pl.pallas_call(
        _attn_context_kernel,
        grid=(BATCH, N_KV_HEADS),
        in_specs=[
            pl.BlockSpec((CTX_LEN, width), lambda b, g: (ctx_blk0 + b, g)),
            pl.BlockSpec((CTX_LEN, HEAD_DIM), lambda b, g: (ctx_blk0 + b, g)),
            pl.BlockSpec((CTX_LEN, 2 * HEAD_DIM), lambda b, g: (ctx_blk0 + b, v_blk0 + g)),
        ],
        out_specs=pl.BlockSpec((CTX_LEN, width), lambda b, g: (b, g)),
        out_shape=jax.ShapeDtypeStruct((N_CTX, ATTN_WIDTH), BF16),
        compiler_params=_params(2),
        name="attention_ctx",
    )(q, kv, kv)
    return attn, attn_ctx


def _merge_kernel(xl_ref, xc_ref, attn_lat_ref, attn_ctx_ref, z_ref, zp_ref, zn_ref, gb_ref, cw_ref, cb_ref,
                  ag_ref, cg_ref, w_ref, g1_ref, sh2_ref, sc2_ref, n2g_ref, xo_ref, h2_ref, cat_scr):
    tm = xo_ref.shape[0]
    i = pl.program_id(0)
    is_ctx = i * tm >= N_LATENT
    attn = jnp.where(is_ctx, attn_ctx_ref[...], attn_lat_ref[...])

    row = lax.broadcasted_iota(jnp.int32, (tm, 1), 0)
    grow = i * tm + row
    seg = jnp.where(grow >= N_LATENT, CTX_LEN, SEQ)
    pos = jnp.bitwise_and(grow, seg - 1)

    z = z_ref[...].astype(F32)
    z_prev = jnp.where(row == 0, zp_ref[HALO - 1:HALO, :].astype(F32), pltpu.roll(z, 1, axis=0))
    z_prev = jnp.where(pos != 0, z_prev, 0.0)
    z_next = jnp.where(row == tm - 1, zn_ref[0:1, :].astype(F32), pltpu.roll(z, tm - 1, axis=0))
    z_next = jnp.where(pos != seg - 1, z_next, 0.0)
    conv = z_prev * cw_ref[0:1, :] + z * cw_ref[1:2, :] + z_next * cw_ref[2:3, :] + cb_ref[...]
    conv = gb_ref[...].astype(F32) * conv

    cat_scr[:, :ATTN_WIDTH] = _rms(attn.astype(F32), ag_ref[...]).astype(BF16)
    cat_scr[:, ATTN_WIDTH:] = _rms(conv, cg_ref[...]).astype(BF16)
    y = jnp.dot(cat_scr[...], w_ref[...], preferred_element_type=F32)
    x_new = jnp.where(is_ctx, xc_ref[...], xl_ref[...]) + g1_ref[...] * y
    xo_ref[...] = x_new
    h2 = _rms(x_new, n2g_ref[...]) * (1.0 + sc2_ref[...]) + sh2_ref[...]
    h2_ref[...] = h2.astype(BF16)


def _merge(stream, attn_lat, attn_ctx, z, gb, mods, layer, cw, cb, ag, cg, w_out, n2g, n_rows):
    tm = MERGE_TM
    hb = tm // HALO
    last_hb = N_TOKENS // HALO - 1
    lat_tiles = N_LATENT // tm
    assert N_LATENT % tm == 0 and N_CTX % tm == 0
    x_lat, x_ctx, ctx_row0 = stream

    def mod_spec(chunk):
        return pl.BlockSpec((None, None, 1, D_MODEL), lambda i: (layer, _mod_row(i, tm), 0, chunk))

    def vec_spec(rows, width):
        return pl.BlockSpec((rows, width), lambda i: (0, 0))

    return pl.pallas_call(
        _merge_kernel,
        grid=(n_rows // tm,),
        in_specs=[
            pl.BlockSpec((tm, D_MODEL), lambda i: (jnp.minimum(i, lat_tiles - 1), 0)),
            pl.BlockSpec((tm, D_MODEL), lambda i: (ctx_row0 // tm + jnp.maximum(i - lat_tiles, 0), 0)),
            pl.BlockSpec((tm, ATTN_WIDTH), lambda i: (jnp.minimum(i, lat_tiles - 1), 0)),
            pl.BlockSpec((tm, ATTN_WIDTH), lambda i: (jnp.maximum(i - lat_tiles, 0), 0)),
            pl.BlockSpec((tm, CONV_WIDTH), lambda i: (i, 0)),
            pl.BlockSpec((HALO, CONV_WIDTH), lambda i: (jnp.maximum(i * hb - 1, 0), 0)),
            pl.BlockSpec((HALO, CONV_WIDTH), lambda i: (jnp.minimum((i + 1) * hb, last_hb), 0)),
            pl.BlockSpec((tm, CONV_WIDTH), lambda i: (i, 0)),
            vec_spec(3, CONV_WIDTH), vec_spec(1, CONV_WIDTH),
            vec_spec(1, ATTN_WIDTH), vec_spec(1, CONV_WIDTH),
            pl.BlockSpec((None, D_MODEL, D_MODEL), lambda i: (layer, 0, 0)),
            mod_spec(2), mod_spec(3), mod_spec(4),
            vec_spec(1, D_MODEL),
        ],
        out_specs=[
            pl.BlockSpec((tm, D_MODEL), lambda i: (i, 0)),
            pl.BlockSpec((tm, D_MODEL), lambda i: (i, 0)),
        ],
        out_shape=[
            jax.ShapeDtypeStruct((n_rows, D_MODEL), F32),
            jax.ShapeDtypeStruct((n_rows, D_MODEL), BF16),
        ],
        scratch_shapes=[pltpu.VMEM((tm, D_MODEL), BF16)],
        compiler_params=_params(1),
        name="merge_out_proj",
    )(x_lat, x_ctx, attn_lat, attn_ctx, z, z, z, gb, cw, cb, ag, cg, w_out, mods, mods, mods, n2g)


def _mlp_kernel(h_ref, w1_ref, w2_ref, x_hbm, g2_ref, o_ref, x_buf, x_sem):
    i, k = pl.program_id(0), pl.program_id(1)
    tm = o_ref.shape[0]
    last_k = pl.num_programs(1) - 1

    def x_copy():
        return pltpu.make_async_copy(x_hbm.at[pl.ds(pl.multiple_of(i * tm, tm), tm), :], x_buf, x_sem)

    @pl.when(k == 0)
    def _():
        x_copy().start()
        o_ref[...] = jnp.zeros_like(o_ref)

    def up(s):
        return jnp.dot(h_ref[...], w1_ref[:, s * MLP_SUB:(s + 1) * MLP_SUB], preferred_element_type=F32)

    n_sub = w1_ref.shape[1] // MLP_SUB
    acts = []
    a_next = up(0)
    for s in range(n_sub):
        a = a_next
        if s + 1 < n_sub:
            a_next = up(s + 1)
        a = jnp.maximum(a, 0.0)
        acts.append((a * a).astype(BF16))
    p = jnp.concatenate(acts, axis=1)
    for n in range(D_MODEL // MLP_OUT_SLAB):
        cols = slice(n * MLP_OUT_SLAB, (n + 1) * MLP_OUT_SLAB)
        o_ref[:, cols] += jnp.dot(p, w2_ref[:, cols], preferred_element_type=F32)

    @pl.when(k == last_k)
    def _():
        x_copy().wait()
        o_ref[...] = x_buf[...] + g2_ref[...] * o_ref[...]


def _mlp(h2, x_new, mods, layer, w1, w2, n_rows):
    tm, th = MLP_TM, MLP_TH
    return pl.pallas_call(
        _mlp_kernel,
        grid=(n_rows // tm, MLP_HIDDEN // th),
        in_specs=[
            pl.BlockSpec((tm, D_MODEL), lambda i, k: (i, 0)),
            pl.BlockSpec((None, D_MODEL, th), lambda i, k: (layer, 0, k)),
            pl.BlockSpec((None, th, D_MODEL), lambda i, k: (layer, k, 0)),
            pl.BlockSpec(memory_space=pl.ANY),
            pl.BlockSpec((None, None, 1, D_MODEL), lambda i, k: (layer, _mod_row(i, tm), 0, 5)),
        ],
        out_specs=pl.BlockSpec((tm, D_MODEL), lambda i, k: (i, 0)),
        out_shape=jax.ShapeDtypeStruct((n_rows, D_MODEL), F32),
        scratch_shapes=[pltpu.VMEM((tm, D_MODEL), F32), pltpu.SemaphoreType.DMA(())],
        compiler_params=_params(2),
        name="mlp",
    )(h2, w1, w2, x_new, mods)


def _rope_tables(tm):
    t = jnp.arange(SEQ)
    row = (t // GRID_W).astype(F32)
    col = (t % GRID_W).astype(F32)
    inv_freq = ROPE_THETA ** (-jnp.arange(0, AXIS_DIM, 2, dtype=F32) / AXIS_DIM)
    ang_r = row[:, None] * inv_freq[None, :]
    ang_c = col[:, None] * inv_freq[None, :]
    ang = jnp.concatenate([ang_r, ang_r, ang_c, ang_c], axis=-1)
    cos, sin = jnp.cos(ang), jnp.sin(ang)
    first = (jnp.arange(HEAD_DIM) % AXIS_DIM) < AXIS_DIM // 2
    sin_a = jnp.where(first, -sin, 0.0)
    sin_b = jnp.where(first, 0.0, sin)
    ident = jnp.zeros((tm, HEAD_DIM), F32)
    return (jnp.concatenate([cos, ident + 1.0]), jnp.concatenate([sin_a, ident]),
            jnp.concatenate([sin_b, ident]))


def kernel(x, c, ctx, c_ctx, w_ada, b_ada, norm1_g, w_in, q_norm_g, k_norm_g, conv_w, conv_b,
           attn_out_g, conv_out_g, w_out, norm2_g, w_mlp_in, w_mlp_out):
    assert x.shape == (BATCH, SEQ, D_MODEL) and ctx.shape == (BATCH, CTX_LEN, D_MODEL)
    stream = (x.reshape(N_LATENT, D_MODEL), ctx.reshape(N_CTX, D_MODEL), 0)
    cond = jnp.concatenate([c, c_ctx[None, :], jnp.zeros((MOD_ROWS - BATCH - 1, D_MODEL), F32)], axis=0)
    mods = _adaln_all(cond, w_ada, b_ada).reshape(DEPTH, MOD_ROWS, 1, N_MOD * D_MODEL)
    cos_t, sa_t, sb_t = _rope_tables(INPROJ_TM)
    w_in, w_out, w_mlp_in, w_mlp_out = (w.astype(BF16) for w in (w_in, w_out, w_mlp_in, w_mlp_out))

    for l in range(DEPTH):
        q, kv, z, gb = _in_proj(stream, mods, l, norm1_g[l][None], w_in,
                                q_norm_g[l][None], k_norm_g[l][None], cos_t, sa_t, sb_t)
        attn_lat, attn_ctx = _attention(q, kv)
        n_rows = N_TOKENS if l < DEPTH - 1 else N_LATENT
        x_new, h2 = _merge(stream, attn_lat, attn_ctx, z, gb, mods, l, conv_w[l], conv_b[l][None],
                           attn_out_g[l][None], conv_out_g[l][None], w_out, norm2_g[l][None], n_rows)
        xs = _mlp(h2, x_new, mods, l, w_mlp_in, w_mlp_out, n_rows)
        stream = (xs, xs, N_LATENT)
    return xs.reshape(BATCH, SEQ, D_MODEL)
```

```python
import functools

import jax
import jax.numpy as jnp
from jax import lax
from jax.experimental import pallas as pl
from jax.experimental.pallas import tpu as pltpu

D_MODEL = 2048
BATCH = 4
SEQ = 4096
DEPTH = 4
CTX_LEN = 256
GRID_W = 64
HEAD_DIM = 128
N_HEADS = 8
N_KV_HEADS = 2
Q_PER_KV = N_HEADS // N_KV_HEADS
ATTN_WIDTH = N_HEADS * HEAD_DIM
KV_WIDTH = N_KV_HEADS * HEAD_DIM
CONV_WIDTH = D_MODEL - ATTN_WIDTH
IN_WIDTH = ATTN_WIDTH + 2 * KV_WIDTH + 3 * CONV_WIDTH
MLP_HIDDEN = 4 * D_MODEL
N_MOD = 6
ROPE_THETA = 10000.0
AXIS_DIM = HEAD_DIM // 2
EPS = 1e-6
LOG2_E = 1.4426950408889634
KV_COLS = KV_WIDTH + 2 * KV_WIDTH

N_LATENT = BATCH * SEQ
N_CTX = BATCH * CTX_LEN
N_TOKENS = N_LATENT + N_CTX
MOD_ROWS = 8

F32 = jnp.float32
BF16 = jnp.bfloat16

ADA_TN = 1024
INPROJ_TM = 1024
INPROJ_TN = 512
ATTN_TQ = 512
ATTN_TK = 512
MERGE_TM = 256
MLP_TM = 1024
MLP_TH = 1024
MLP_SUB = 512
MLP_OUT_SLAB = 512
HALO = 16
VMEM_LIMIT = 56 * 1024 * 1024


def _params(n_axes, vmem=VMEM_LIMIT):
    return pltpu.CompilerParams(dimension_semantics=("arbitrary",) * n_axes, vmem_limit_bytes=vmem)


def _mod_row(tile, tm):
    return jnp.minimum(tile // (SEQ // tm), BATCH)


def _rms(x, g):
    return x * lax.rsqrt(jnp.mean(x * x, axis=-1, keepdims=True) + EPS) * g


def _ada_kernel(cond_ref, w_ref, b_ref, o_ref):
    cnd = cond_ref[...]
    s = (cnd * jax.nn.sigmoid(cnd)).astype(BF16)
    w = w_ref[...].astype(BF16)
    o_ref[...] = jnp.dot(s, w, preferred_element_type=F32) + b_ref[...]


def _adaln_all(cond, w_ada, b_ada):
    n = N_MOD * D_MODEL
    return pl.pallas_call(
        _ada_kernel,
        grid=(DEPTH, n // ADA_TN),
        in_specs=[
            pl.BlockSpec((MOD_ROWS, D_MODEL), lambda l, j: (0, 0)),
            pl.BlockSpec((None, D_MODEL, ADA_TN), lambda l, j: (l, 0, j)),
            pl.BlockSpec((None, 1, ADA_TN), lambda l, j: (l, 0, j)),
        ],
        out_specs=pl.BlockSpec((None, MOD_ROWS, ADA_TN), lambda l, j: (l, 0, j)),
        out_shape=jax.ShapeDtypeStruct((DEPTH, MOD_ROWS, n), F32),
        compiler_params=_params(2),
        name="adaln",
    )(cond, w_ada, b_ada.reshape(DEPTH, 1, n))


_KVB = ATTN_WIDTH // INPROJ_TN
_GBB = (ATTN_WIDTH + 2 * KV_WIDTH) // INPROJ_TN
_GCB = _GBB + CONV_WIDTH // INPROJ_TN
_UB = _GCB + CONV_WIDTH // INPROJ_TN
assert _KVB == 2 and _GCB - _GBB == 2 and 2 * KV_WIDTH == INPROJ_TN
_PAIRS = ((0, 1), (_KVB, _GCB), (_UB, _GCB + 1), (_UB + 1, _GBB), (_GBB + 1, None))
_NP = len(_PAIRS)
_SLOT = tuple(j % 3 for j in range(_NP))
_NORM_ROWS = INPROJ_TM // (_NP - 1)


def _lookup(values, idx):
    out = jnp.int32(values[0])
    for n, v in enumerate(values[1:], start=1):
        out = jnp.where(idx == n, jnp.int32(v), out)
    return out


def _rope(x, cos, sin_a, sin_b):
    return (x * cos + pltpu.roll(x, HEAD_DIM - AXIS_DIM // 2, axis=1) * sin_a
            + pltpu.roll(x, AXIS_DIM // 2, axis=1) * sin_b)


def _inproj_kernel(xl_ref, xc_ref, sh_ref, sc_ref, n1g_ref, wa_ref, wb_ref, qg_ref, kg_ref, cos_ref, sa_ref, sb_ref,
                   q_ref, kv_ref, z_ref, gb_ref, h_cur, h_next, ra0, ra1, ra2, rb0, rb1, rb2, *, split_stream):
    t = pl.program_id(0)
    j = t % _NP
    tm = q_ref.shape[0]
    ra, rb = (ra0, ra1, ra2), (rb0, rb1, rb2)

    def normalised():
        x = xl_ref[...]
        if split_stream:
            norm_tile = jnp.minimum(t // _NP, N_TOKENS // tm - 1)
            x = jnp.where(norm_tile * tm >= N_LATENT, xc_ref[...], x)
        h = _rms(x, n1g_ref[...]) * (1.0 + sc_ref[...]) + sh_ref[...]
        return h.astype(BF16)

    def head(r_ref, col0, gain, scale):
        y = _rms(r_ref[:, col0:col0 + HEAD_DIM], gain)
        y = _rope(y, cos_ref[...], sa_ref[...], sb_ref[...])
        return (y * scale).astype(BF16)

    def multiply(step):
        slot = _SLOT[step]
        ra[slot][...] = jnp.dot(h_cur[...], wa_ref[...], preferred_element_type=F32)
        if _PAIRS[step][1] is not None:
            rb[slot][...] = jnp.dot(h_cur[...], wb_ref[...], preferred_element_type=F32)

    def finish(step):
        if step == 1:
            for half, r_ref in enumerate((ra[_SLOT[0]], rb[_SLOT[0]])):
                for hd in range(INPROJ_TN // HEAD_DIM):
                    c0 = half * INPROJ_TN + hd * HEAD_DIM
                    q_ref[:, c0:c0 + HEAD_DIM] = head(r_ref, hd * HEAD_DIM, qg_ref[...],
                                                      HEAD_DIM ** -0.5 * LOG2_E)
        elif step == 2:
            r_ref = ra[_SLOT[1]]
            for hd in range(N_KV_HEADS):
                kv_ref[:, hd * HEAD_DIM:(hd + 1) * HEAD_DIM] = head(r_ref, hd * HEAD_DIM, kg_ref[...], 1.0)
                v0 = KV_WIDTH + hd * 2 * HEAD_DIM
                kv_ref[:, v0:v0 + HEAD_DIM] = r_ref[:, KV_WIDTH + hd * HEAD_DIM:
                                                    KV_WIDTH + (hd + 1) * HEAD_DIM].astype(BF16)
                kv_ref[:, v0 + HEAD_DIM:v0 + 2 * HEAD_DIM] = jnp.ones((tm, HEAD_DIM), BF16)
        elif step == 3:
            z_ref[...] = (rb[_SLOT[1]][...] * ra[_SLOT[2]][...]).astype(BF16)
        elif step == 4:
            z_ref[...] = (rb[_SLOT[2]][...] * ra[_SLOT[3]][...]).astype(BF16)
            gb_ref[...] = rb[_SLOT[3]][...].astype(BF16)
        elif step == 0:
            gb_ref[...] = ra[_SLOT[4]][...].astype(BF16)

    @pl.when(jnp.logical_and(t >= 1, t < _NP))
    def _():
        r0 = pl.multiple_of((t - 1) * _NORM_ROWS, _NORM_ROWS)
        h_next[pl.ds(r0, _NORM_ROWS), :] = normalised()

    @pl.when(t == _NP - 1)
    def _():
        h_cur[...] = h_next[...]

    @pl.when(t == _NP)
    def _():
        multiply(0)

    for step in range(_NP):
        @pl.when(jnp.logical_and(t > _NP, j == step))
        def _(step=step):
            multiply(step)
            if step >= 1:
                h_next[(step - 1) * _NORM_ROWS:step * _NORM_ROWS, :] = normalised()
            finish(step)
            if step == _NP - 1:
                h_cur[...] = h_next[...]


def _in_proj(stream, mods, layer, n1g, w_in, qg, kg, cos_t, sa_t, sb_t):
    tm, tn = INPROJ_TM, INPROJ_TN
    n_tiles = N_TOKENS // tm
    lat_tiles = SEQ // tm
    slices = _NP - 1
    assert tm % slices == 0 and _NORM_ROWS % 16 == 0
    x_lat, x_ctx = stream
    split = x_ctx is not None
    lat_blocks = N_LATENT // _NORM_ROWS

    def tile_of(t):
        return t // _NP - 1

    def norm_block(t):
        tile = jnp.clip(tile_of(t) + 1, 0, n_tiles - 1)
        return tile * slices + jnp.clip(t % _NP - 1, 0, slices - 1)

    def fin_tile(t):
        return jnp.clip(tile_of(t) - (t % _NP == 0).astype(jnp.int32), 0, n_tiles - 1)

    def mod_spec(chunk):
        return pl.BlockSpec((None, None, 1, D_MODEL),
                            lambda t: (layer, _mod_row(norm_block(t) // slices, tm), 0, chunk))

    def w_spec(side):
        blocks = [p[side] if p[side] is not None else p[0] for p in _PAIRS]
        return pl.BlockSpec((None, D_MODEL, tn), lambda t: (layer, 0, _lookup(blocks, t % _NP)))

    def tab_spec():
        def index(t):
            i = fin_tile(t)
            return (jnp.where(i < N_LATENT // tm, i % lat_tiles, lat_tiles), 0)
        return pl.BlockSpec((tm, HEAD_DIM), index)

    def second_col(t, first_step):
        j = t % _NP
        return jnp.logical_and(t > _NP, jnp.logical_or(j == 0, j > first_step)).astype(jnp.int32)

    if split:
        x_specs = [pl.BlockSpec((_NORM_ROWS, D_MODEL), lambda t: (jnp.minimum(norm_block(t), lat_blocks - 1), 0)),
                   pl.BlockSpec((_NORM_ROWS, D_MODEL), lambda t: (jnp.maximum(norm_block(t) - lat_blocks, 0), 0))]
    else:
        x_specs = [pl.BlockSpec((_NORM_ROWS, D_MODEL), lambda t: (norm_block(t), 0)),
                   pl.BlockSpec((_NORM_ROWS, D_MODEL), lambda t: (0, 0))]
        x_ctx = x_lat

    return pl.pallas_call(
        functools.partial(_inproj_kernel, split_stream=split),
        grid=((n_tiles + 1) * _NP + 1,),
        in_specs=x_specs + [
            mod_spec(0), mod_spec(1),
            pl.BlockSpec((1, D_MODEL), lambda t: (0, 0)),
            w_spec(0), w_spec(1),
            pl.BlockSpec((1, HEAD_DIM), lambda t: (0, 0)),
            pl.BlockSpec((1, HEAD_DIM), lambda t: (0, 0)),
            tab_spec(), tab_spec(), tab_spec(),
        ],
        out_specs=[
            pl.BlockSpec((tm, ATTN_WIDTH), lambda t: (fin_tile(t), 0)),
            pl.BlockSpec((tm, KV_COLS), lambda t: (fin_tile(t), 0)),
            pl.BlockSpec((tm, tn), lambda t: (fin_tile(t), second_col(t, 3))),
            pl.BlockSpec((tm, tn), lambda t: (fin_tile(t), second_col(t, 4))),
        ],
        out_shape=[
            jax.ShapeDtypeStruct((N_TOKENS, ATTN_WIDTH), BF16),
            jax.ShapeDtypeStruct((N_TOKENS, KV_COLS), BF16),
            jax.ShapeDtypeStruct((N_TOKENS, CONV_WIDTH), BF16),
            jax.ShapeDtypeStruct((N_TOKENS, CONV_WIDTH), BF16),
        ],
        scratch_shapes=[pltpu.VMEM((tm, D_MODEL), BF16)] * 2 + [pltpu.VMEM((tm, tn), F32)] * 6,
        compiler_params=_params(1),
        name="in_proj",
    )(x_lat, x_ctx, mods, mods, n1g, w_in, w_in, qg, kg, cos_t, sa_t, sb_t)


_NQ = SEQ // ATTN_TQ


def _attend(q, chunks):
    def scores(chunk):
        k_ref, _, start, size = chunk
        return lax.dot_general(q, k_ref[start:start + size, :], (((1,), (1,)), ((), ())),
                               preferred_element_type=F32)

    m = acc = None
    s = scores(chunks[0])
    for idx, (_, v_ref, start, size) in enumerate(chunks):
        s_next = scores(chunks[idx + 1]) if idx + 1 < len(chunks) else None
        m_c = jnp.max(s, axis=-1, keepdims=True)
        m_new = m_c if m is None else jnp.maximum(m, m_c)
        p = jnp.exp2(s - m_new).astype(BF16)
        pv = jnp.dot(p, v_ref[start:start + size, :], preferred_element_type=F32)
        acc = pv if m is None else jnp.exp2(m - m_new) * acc + pv
        m, s = m_new, s_next
    return acc[:, :HEAD_DIM] / acc[:, HEAD_DIM:]


def _attend_heads(q_ref, o_ref, chunks):
    tq = q_ref.shape[0]
    q = jnp.concatenate([q_ref[:, r * HEAD_DIM:(r + 1) * HEAD_DIM] for r in range(Q_PER_KV)], axis=0)
    o = _attend(q, chunks).astype(BF16)
    for r in range(Q_PER_KV):
        o_ref[:, r * HEAD_DIM:(r + 1) * HEAD_DIM] = o[r * tq:(r + 1) * tq, :]


def _attn_latent_kernel(q_ref, kl_ref, vl_ref, kc_ref, vc_ref, o_ref):
    chunks = [(kl_ref, vl_ref, c * ATTN_TK, ATTN_TK) for c in range(SEQ // ATTN_TK)]
    _attend_heads(q_ref, o_ref, chunks + [(kc_ref, vc_ref, 0, CTX_LEN)])


def _attn_context_kernel(q_ref, kc_ref, vc_ref, o_ref):
    _attend_heads(q_ref, o_ref, [(kc_ref, vc_ref, 0, CTX_LEN)])


def _attention(q, kv):
    tq = ATTN_TQ
    ctx_blk0 = N_LATENT // CTX_LEN
    v_blk0 = KV_WIDTH // (2 * HEAD_DIM)
    width = Q_PER_KV * HEAD_DIM

    attn = pl.pallas_call(
        _attn_latent_kernel,
        grid=(BATCH, N_KV_HEADS, _NQ),
        in_specs=[
            pl.BlockSpec((tq, width), lambda b, g, qi: (b * _NQ + qi, g)),
            pl.BlockSpec((SEQ, HEAD_DIM), lambda b, g, qi: (b, g)),
            pl.BlockSpec((SEQ, 2 * HEAD_DIM), lambda b, g, qi: (b, v_blk0 + g)),
            pl.BlockSpec((CTX_LEN, HEAD_DIM), lambda b, g, qi: (ctx_blk0 + b, g)),
            pl.BlockSpec((CTX_LEN, 2 * HEAD_DIM), lambda b, g, qi: (ctx_blk0 + b, v_blk0 + g)),
        ],
        out_specs=pl.BlockSpec((tq, width), lambda b, g, qi: (b * _NQ + qi, g)),
        out_shape=jax.ShapeDtypeStruct((N_LATENT, ATTN_WIDTH), BF16),
        compiler_params=_params(3),
        name="attention",
    )(q, kv, kv, kv, kv)
    attn_ctx = pl.pallas_call(
        _attn_context_kernel,
        grid=(BATCH, N_KV_HEADS),
        in_specs=[
            pl.BlockSpec((CTX_LEN, width), lambda b, g: (ctx_blk0 + b, g)),
            pl.BlockSpec((CTX_LEN, HEAD_DIM), lambda b, g: (ctx_blk0 + b, g)),
            pl.BlockSpec((CTX_LEN, 2 * HEAD_DIM), lambda b, g: (ctx_blk0 + b, v_blk0 + g)),
        ],
        out_specs=pl.BlockSpec((CTX_LEN, width), lambda b, g: (b, g)),
        out_shape=jax.ShapeDtypeStruct((N_CTX, ATTN_WIDTH), BF16),
        compiler_params=_params(2),
        name="attention_ctx",
    )(q, kv, kv)
    return attn, attn_ctx


def _merge_kernel(xl_ref, xc_ref, attn_lat_ref, attn_ctx_ref, z_ref, zp_ref, zn_ref, gb_ref, cw_ref, cb_ref,
                  ag_ref, cg_ref, w_ref, g1_ref, sh2_ref, sc2_ref, n2g_ref, xo_ref, h2_ref, cat_scr, *, split_stream):
    tm = xo_ref.shape[0]
    i = pl.program_id(0)
    is_ctx = i * tm >= N_LATENT
    attn = jnp.where(is_ctx, attn_ctx_ref[...], attn_lat_ref[...])

    row = lax.broadcasted_iota(jnp.int32, (tm, 1), 0)
    grow = i * tm + row
    seg = jnp.where(grow >= N_LATENT, CTX_LEN, SEQ)
    pos = jnp.bitwise_and(grow, seg - 1)

    z = z_ref[...].astype(F32)
    z_prev = jnp.where(row == 0, zp_ref[HALO - 1:HALO, :].astype(F32), pltpu.roll(z, 1, axis=0))
    z_prev = jnp.where(pos != 0, z_prev, 0.0)
    z_next = jnp.where(row == tm - 1, zn_ref[0:1, :].astype(F32), pltpu.roll(z, tm - 1, axis=0))
    z_next = jnp.where(pos != seg - 1, z_next, 0.0)
    conv = z_prev * cw_ref[0:1, :] + z * cw_ref[1:2, :] + z_next * cw_ref[2:3, :] + cb_ref[...]
    conv = gb_ref[...].astype(F32) * conv

    cat_scr[:, :ATTN_WIDTH] = _rms(attn.astype(F32), ag_ref[...]).astype(BF16)
    cat_scr[:, ATTN_WIDTH:] = _rms(conv, cg_ref[...]).astype(BF16)
    y = jnp.dot(cat_scr[...], w_ref[...], preferred_element_type=F32)
    x = jnp.where(is_ctx, xc_ref[...], xl_ref[...]) if split_stream else xl_ref[...]
    x_new = x + g1_ref[...] * y
    xo_ref[...] = x_new
    h2 = _rms(x_new, n2g_ref[...]) * (1.0 + sc2_ref[...]) + sh2_ref[...]
    h2_ref[...] = h2.astype(BF16)


def _merge(stream, attn_lat, attn_ctx, z, gb, mods, layer, cw, cb, ag, cg, w_out, n2g, n_rows):
    tm = MERGE_TM
    hb = tm // HALO
    last_hb = N_TOKENS // HALO - 1
    lat_tiles = N_LATENT // tm
    assert N_LATENT % tm == 0 and N_CTX % tm == 0
    x_lat, x_ctx = stream
    split = x_ctx is not None
    if split:
        x_specs = [pl.BlockSpec((tm, D_MODEL), lambda i: (jnp.minimum(i, lat_tiles - 1), 0)),
                   pl.BlockSpec((tm, D_MODEL), lambda i: (jnp.maximum(i - lat_tiles, 0), 0))]
    else:
        x_specs = [pl.BlockSpec((tm, D_MODEL), lambda i: (i, 0)), pl.BlockSpec((tm, D_MODEL), lambda i: (0, 0))]
        x_ctx = x_lat

    def mod_spec(chunk):
        return pl.BlockSpec((None, None, 1, D_MODEL), lambda i: (layer, _mod_row(i, tm), 0, chunk))

    def vec_spec(rows, width):
        return pl.BlockSpec((rows, width), lambda i: (0, 0))

    return pl.pallas_call(
        functools.partial(_merge_kernel, split_stream=split),
        grid=(n_rows // tm,),
        in_specs=x_specs + [
            pl.BlockSpec((tm, ATTN_WIDTH), lambda i: (jnp.minimum(i, lat_tiles - 1), 0)),
            pl.BlockSpec((tm, ATTN_WIDTH), lambda i: (jnp.maximum(i - lat_tiles, 0), 0)),
            pl.BlockSpec((tm, CONV_WIDTH), lambda i: (i, 0)),
            pl.BlockSpec((HALO, CONV_WIDTH), lambda i: (jnp.maximum(i * hb - 1, 0), 0)),
            pl.BlockSpec((HALO, CONV_WIDTH), lambda i: (jnp.minimum((i + 1) * hb, last_hb), 0)),
            pl.BlockSpec((tm, CONV_WIDTH), lambda i: (i, 0)),
            vec_spec(3, CONV_WIDTH), vec_spec(1, CONV_WIDTH),
            vec_spec(1, ATTN_WIDTH), vec_spec(1, CONV_WIDTH),
            pl.BlockSpec((None, D_MODEL, D_MODEL), lambda i: (layer, 0, 0)),
            mod_spec(2), mod_spec(3), mod_spec(4),
            vec_spec(1, D_MODEL),
        ],
        out_specs=[
            pl.BlockSpec((tm, D_MODEL), lambda i: (i, 0)),
            pl.BlockSpec((tm, D_MODEL), lambda i: (i, 0)),
        ],
        out_shape=[
            jax.ShapeDtypeStruct((n_rows, D_MODEL), F32),
            jax.ShapeDtypeStruct((n_rows, D_MODEL), BF16),
        ],
        scratch_shapes=[pltpu.VMEM((tm, D_MODEL), BF16)],
        compiler_params=_params(1),
        name="merge_out_proj",
    )(x_lat, x_ctx, attn_lat, attn_ctx, z, z, z, gb, cw, cb, ag, cg, w_out, mods, mods, mods, n2g)


def _mlp_kernel(h_ref, w1_ref, w2_ref, x_hbm, g2_ref, o_ref, x_buf, x_sem):
    i, k = pl.program_id(0), pl.program_id(1)
    tm = o_ref.shape[0]
    last_k = pl.num_programs(1) - 1

    def x_copy():
        return pltpu.make_async_copy(x_hbm.at[pl.ds(pl.multiple_of(i * tm, tm), tm), :], x_buf, x_sem)

    @pl.when(k == 0)
    def _():
        x_copy().start()
        o_ref[...] = jnp.zeros_like(o_ref)

    def up(s):
        return jnp.dot(h_ref[...], w1_ref[:, s * MLP_SUB:(s + 1) * MLP_SUB], preferred_element_type=F32)

    n_sub = w1_ref.shape[1] // MLP_SUB
    acts = []
    a_next = up(0)
    for s in range(n_sub):
        a = a_next
        if s + 1 < n_sub:
            a_next = up(s + 1)
        a = jnp.maximum(a, 0.0)
        acts.append((a * a).astype(BF16))
    p = jnp.concatenate(acts, axis=1)
    for n in range(D_MODEL // MLP_OUT_SLAB):
        cols = slice(n * MLP_OUT_SLAB, (n + 1) * MLP_OUT_SLAB)
        o_ref[:, cols] += jnp.dot(p, w2_ref[:, cols], preferred_element_type=F32)

    @pl.when(k == last_k)
    def _():
        x_copy().wait()
        o_ref[...] = x_buf[...] + g2_ref[...] * o_ref[...]


def _mlp(h2, x_new, mods, layer, w1, w2, n_rows):
    tm, th = MLP_TM, MLP_TH
    return pl.pallas_call(
        _mlp_kernel,
        grid=(n_rows // tm, MLP_HIDDEN // th),
        in_specs=[
            pl.BlockSpec((tm, D_MODEL), lambda i, k: (i, 0)),
            pl.BlockSpec((None, D_MODEL, th), lambda i, k: (layer, 0, k)),
            pl.BlockSpec((None, th, D_MODEL), lambda i, k: (layer, k, 0)),
            pl.BlockSpec(memory_space=pl.ANY),
            pl.BlockSpec((None, None, 1, D_MODEL), lambda i, k: (layer, _mod_row(i, tm), 0, 5)),
        ],
        out_specs=pl.BlockSpec((tm, D_MODEL), lambda i, k: (i, 0)),
        out_shape=jax.ShapeDtypeStruct((n_rows, D_MODEL), F32),
        scratch_shapes=[pltpu.VMEM((tm, D_MODEL), F32), pltpu.SemaphoreType.DMA(())],
        compiler_params=_params(2),
        name="mlp",
    )(h2, w1, w2, x_new, mods)


def _rope_tables(tm):
    t = jnp.arange(SEQ)
    row = (t // GRID_W).astype(F32)
    col = (t % GRID_W).astype(F32)
    inv_freq = ROPE_THETA ** (-jnp.arange(0, AXIS_DIM, 2, dtype=F32) / AXIS_DIM)
    ang_r = row[:, None] * inv_freq[None, :]
    ang_c = col[:, None] * inv_freq[None, :]
    ang = jnp.concatenate([ang_r, ang_r, ang_c, ang_c], axis=-1)
    cos, sin = jnp.cos(ang), jnp.sin(ang)
    first = (jnp.arange(HEAD_DIM) % AXIS_DIM) < AXIS_DIM // 2
    sin_a = jnp.where(first, -sin, 0.0)
    sin_b = jnp.where(first, 0.0, sin)
    ident = jnp.zeros((tm, HEAD_DIM), F32)
    return (jnp.concatenate([cos, ident + 1.0]), jnp.concatenate([sin_a, ident]),
            jnp.concatenate([sin_b, ident]))


def kernel(x, c, ctx, c_ctx, w_ada, b_ada, norm1_g, w_in, q_norm_g, k_norm_g, conv_w, conv_b,
           attn_out_g, conv_out_g, w_out, norm2_g, w_mlp_in, w_mlp_out):
    assert x.shape == (BATCH, SEQ, D_MODEL) and ctx.shape == (BATCH, CTX_LEN, D_MODEL)
    stream = (x.reshape(N_LATENT, D_MODEL), ctx.reshape(N_CTX, D_MODEL))
    cond = jnp.concatenate([c, c_ctx[None, :], jnp.zeros((MOD_ROWS - BATCH - 1, D_MODEL), F32)], axis=0)
    mods = _adaln_all(cond, w_ada, b_ada).reshape(DEPTH, MOD_ROWS, 1, N_MOD * D_MODEL)
    cos_t, sa_t, sb_t = _rope_tables(INPROJ_TM)
    w_in, w_out, w_mlp_in, w_mlp_out = (w.astype(BF16) for w in (w_in, w_out, w_mlp_in, w_mlp_out))

    for l in range(DEPTH):
        q, kv, z, gb = _in_proj(stream, mods, l, norm1_g[l][None], w_in,
                                q_norm_g[l][None], k_norm_g[l][None], cos_t, sa_t, sb_t)
        attn_lat, attn_ctx = _attention(q, kv)
        n_rows = N_TOKENS if l < DEPTH - 1 else N_LATENT
        x_new, h2 = _merge(stream, attn_lat, attn_ctx, z, gb, mods, l, conv_w[l], conv_b[l][None],
                           attn_out_g[l][None], conv_out_g[l][None], w_out, norm2_g[l][None], n_rows)
        xs = _mlp(h2, x_new, mods, l, w_mlp_in, w_mlp_out, n_rows)
        stream = (xs, None)
    return xs.reshape(BATCH, SEQ, D_MODEL)
```

```python
import functools

import jax
import jax.numpy as jnp
from jax import lax
from jax.experimental import pallas as pl
from jax.experimental.pallas import tpu as pltpu

D_MODEL = 2048
BATCH = 4
SEQ = 4096
DEPTH = 4
CTX_LEN = 256
GRID_W = 64
HEAD_DIM = 128
N_HEADS = 8
N_KV_HEADS = 2
Q_PER_KV = N_HEADS // N_KV_HEADS
ATTN_WIDTH = N_HEADS * HEAD_DIM
KV_WIDTH = N_KV_HEADS * HEAD_DIM
CONV_WIDTH = D_MODEL - ATTN_WIDTH
IN_WIDTH = ATTN_WIDTH + 2 * KV_WIDTH + 3 * CONV_WIDTH
MLP_HIDDEN = 4 * D_MODEL
N_MOD = 6
ROPE_THETA = 10000.0
AXIS_DIM = HEAD_DIM // 2
EPS = 1e-6
LOG2_E = 1.4426950408889634
KV_COLS = KV_WIDTH + 2 * KV_WIDTH

N_LATENT = BATCH * SEQ
N_CTX = BATCH * CTX_LEN
N_TOKENS = N_LATENT + N_CTX
MOD_ROWS = 8

F32 = jnp.float32
BF16 = jnp.bfloat16

ADA_TN = 1024
INPROJ_TM = 1024
INPROJ_TN = 512
ATTN_TQ = 512
ATTN_TK = 512
MERGE_TM = 512
MLP_TM = 1024
MLP_TH = 1024
MLP_SUB = 512
MLP_OUT_SLAB = 512
HALO = 16
VMEM_LIMIT = 56 * 1024 * 1024


def _params(n_axes, vmem=VMEM_LIMIT):
    return pltpu.CompilerParams(dimension_semantics=("arbitrary",) * n_axes, vmem_limit_bytes=vmem)


def _mod_row(tile, tm):
    return jnp.minimum(tile // (SEQ // tm), BATCH)


def _rms(x, g):
    return x * lax.rsqrt(jnp.mean(x * x, axis=-1, keepdims=True) + EPS) * g


def _ada_kernel(cond_ref, w_ref, b_ref, o_ref):
    cnd = cond_ref[...]
    s = (cnd * jax.nn.sigmoid(cnd)).astype(BF16)
    w = w_ref[...].astype(BF16)
    o_ref[...] = jnp.dot(s, w, preferred_element_type=F32) + b_ref[...]


def _adaln_all(cond, w_ada, b_ada):
    n = N_MOD * D_MODEL
    return pl.pallas_call(
        _ada_kernel,
        grid=(DEPTH, n // ADA_TN),
        in_specs=[
            pl.BlockSpec((MOD_ROWS, D_MODEL), lambda l, j: (0, 0)),
            pl.BlockSpec((None, D_MODEL, ADA_TN), lambda l, j: (l, 0, j)),
            pl.BlockSpec((None, 1, ADA_TN), lambda l, j: (l, 0, j)),
        ],
        out_specs=pl.BlockSpec((None, MOD_ROWS, ADA_TN), lambda l, j: (l, 0, j)),
        out_shape=jax.ShapeDtypeStruct((DEPTH, MOD_ROWS, n), F32),
        compiler_params=_params(2),
        name="adaln",
    )(cond, w_ada, b_ada.reshape(DEPTH, 1, n))


_KVB = ATTN_WIDTH // INPROJ_TN
_GBB = (ATTN_WIDTH + 2 * KV_WIDTH) // INPROJ_TN
_GCB = _GBB + CONV_WIDTH // INPROJ_TN
_UB = _GCB + CONV_WIDTH // INPROJ_TN
assert _KVB == 2 and _GCB - _GBB == 2 and 2 * KV_WIDTH == INPROJ_TN
_PAIRS = ((0, 1), (_KVB, _GCB), (_UB, _GCB + 1), (_UB + 1, _GBB), (_GBB + 1, None))
_NP = len(_PAIRS)
_SLOT = tuple(j % 3 for j in range(_NP))
_NORM_ROWS = INPROJ_TM // (_NP - 1)


def _lookup(values, idx):
    out = jnp.int32(values[0])
    for n, v in enumerate(values[1:], start=1):
        out = jnp.where(idx == n, jnp.int32(v), out)
    return out


def _rope(x, cos, sin_a, sin_b):
    return (x * cos + pltpu.roll(x, HEAD_DIM - AXIS_DIM // 2, axis=1) * sin_a
            + pltpu.roll(x, AXIS_DIM // 2, axis=1) * sin_b)


def _inproj_kernel(xl_ref, xc_ref, sh_ref, sc_ref, n1g_ref, wa_ref, wb_ref, qg_ref, kg_ref, cos_ref, sa_ref, sb_ref,
                   q_ref, kv_ref, z_ref, gb_ref, h_cur, h_next, ra0, ra1, ra2, rb0, rb1, rb2, *, split_stream):
    t = pl.program_id(0)
    j = t % _NP
    tm = q_ref.shape[0]
    ra, rb = (ra0, ra1, ra2), (rb0, rb1, rb2)

    def normalised():
        x = xl_ref[...]
        if split_stream:
            norm_tile = jnp.minimum(t // _NP, N_TOKENS // tm - 1)
            x = jnp.where(norm_tile * tm >= N_LATENT, xc_ref[...], x)
        h = _rms(x, n1g_ref[...]) * (1.0 + sc_ref[...]) + sh_ref[...]
        return h.astype(BF16)

    def head(r_ref, col0, gain, scale):
        y = _rms(r_ref[:, col0:col0 + HEAD_DIM], gain)
        y = _rope(y, cos_ref[...], sa_ref[...], sb_ref[...])
        return (y * scale).astype(BF16)

    def multiply(step):
        slot = _SLOT[step]
        ra[slot][...] = jnp.dot(h_cur[...], wa_ref[...], preferred_element_type=F32)
        if _PAIRS[step][1] is not None:
            rb[slot][...] = jnp.dot(h_cur[...], wb_ref[...], preferred_element_type=F32)

    def finish(step):
        if step == 1:
            for half, r_ref in enumerate((ra[_SLOT[0]], rb[_SLOT[0]])):
                for hd in range(INPROJ_TN // HEAD_DIM):
                    c0 = half * INPROJ_TN + hd * HEAD_DIM
                    q_ref[:, c0:c0 + HEAD_DIM] = head(r_ref, hd * HEAD_DIM, qg_ref[...],
                                                      HEAD_DIM ** -0.5 * LOG2_E)
        elif step == 2:
            r_ref = ra[_SLOT[1]]
            for hd in range(N_KV_HEADS):
                kv_ref[:, hd * HEAD_DIM:(hd + 1) * HEAD_DIM] = head(r_ref, hd * HEAD_DIM, kg_ref[...], 1.0)
                v0 = KV_WIDTH + hd * 2 * HEAD_DIM
                kv_ref[:, v0:v0 + HEAD_DIM] = r_ref[:, KV_WIDTH + hd * HEAD_DIM:
                                                    KV_WIDTH + (hd + 1) * HEAD_DIM].astype(BF16)
                kv_ref[:, v0 + HEAD_DIM:v0 + 2 * HEAD_DIM] = jnp.ones((tm, HEAD_DIM), BF16)
        elif step == 3:
            z_ref[...] = (rb[_SLOT[1]][...] * ra[_SLOT[2]][...]).astype(BF16)
        elif step == 4:
            z_ref[...] = (rb[_SLOT[2]][...] * ra[_SLOT[3]][...]).astype(BF16)
            gb_ref[...] = rb[_SLOT[3]][...].astype(BF16)
        elif step == 0:
            gb_ref[...] = ra[_SLOT[4]][...].astype(BF16)

    @pl.when(jnp.logical_and(t >= 1, t < _NP))
    def _():
        r0 = pl.multiple_of((t - 1) * _NORM_ROWS, _NORM_ROWS)
        h_next[pl.ds(r0, _NORM_ROWS), :] = normalised()

    @pl.when(t == _NP - 1)
    def _():
        h_cur[...] = h_next[...]

    @pl.when(t == _NP)
    def _():
        multiply(0)

    for step in range(_NP):
        @pl.when(jnp.logical_and(t > _NP, j == step))
        def _(step=step):
            multiply(step)
            if step >= 1:
                h_next[(step - 1) * _NORM_ROWS:step * _NORM_ROWS, :] = normalised()
            finish(step)
            if step == _NP - 1:
                h_cur[...] = h_next[...]


def _in_proj(stream, mods, layer, n1g, w_in, qg, kg, cos_t, sa_t, sb_t):
    tm, tn = INPROJ_TM, INPROJ_TN
    n_tiles = N_TOKENS // tm
    lat_tiles = SEQ // tm
    slices = _NP - 1
    assert tm % slices == 0 and _NORM_ROWS % 16 == 0
    x_lat, x_ctx = stream
    split = x_ctx is not None
    lat_blocks = N_LATENT // _NORM_ROWS

    def tile_of(t):
        return t // _NP - 1

    def norm_block(t):
        tile = jnp.clip(tile_of(t) + 1, 0, n_tiles - 1)
        return tile * slices + jnp.clip(t % _NP - 1, 0, slices - 1)

    def fin_tile(t):
        return jnp.clip(tile_of(t) - (t % _NP == 0).astype(jnp.int32), 0, n_tiles - 1)

    def mod_spec(chunk):
        return pl.BlockSpec((None, None, 1, D_MODEL),
                            lambda t: (layer, _mod_row(norm_block(t) // slices, tm), 0, chunk))

    def w_spec(side):
        blocks = [p[side] if p[side] is not None else p[0] for p in _PAIRS]
        return pl.BlockSpec((None, D_MODEL, tn), lambda t: (layer, 0, _lookup(blocks, t % _NP)))

    def tab_spec():
        def index(t):
            i = fin_tile(t)
            return (jnp.where(i < N_LATENT // tm, i % lat_tiles, lat_tiles), 0)
        return pl.BlockSpec((tm, HEAD_DIM), index)

    def second_col(t, first_step):
        j = t % _NP
        return jnp.logical_and(t > _NP, jnp.logical_or(j == 0, j > first_step)).astype(jnp.int32)

    if split:
        x_specs = [pl.BlockSpec((_NORM_ROWS, D_MODEL), lambda t: (jnp.minimum(norm_block(t), lat_blocks - 1), 0)),
                   pl.BlockSpec((_NORM_ROWS, D_MODEL), lambda t: (jnp.maximum(norm_block(t) - lat_blocks, 0), 0))]
    else:
        x_specs = [pl.BlockSpec((_NORM_ROWS, D_MODEL), lambda t: (norm_block(t), 0)),
                   pl.BlockSpec((HALO, D_MODEL), lambda t: (0, 0))]
        x_ctx = x_lat

    return pl.pallas_call(
        functools.partial(_inproj_kernel, split_stream=split),
        grid=((n_tiles + 1) * _NP + 1,),
        in_specs=x_specs + [
            mod_spec(0), mod_spec(1),
            pl.BlockSpec((1, D_MODEL), lambda t: (0, 0)),
            w_spec(0), w_spec(1),
            pl.BlockSpec((1, HEAD_DIM), lambda t: (0, 0)),
            pl.BlockSpec((1, HEAD_DIM), lambda t: (0, 0)),
            tab_spec(), tab_spec(), tab_spec(),
        ],
        out_specs=[
            pl.BlockSpec((tm, ATTN_WIDTH), lambda t: (fin_tile(t), 0)),
            pl.BlockSpec((tm, KV_COLS), lambda t: (fin_tile(t), 0)),
            pl.BlockSpec((tm, tn), lambda t: (fin_tile(t), second_col(t, 3))),
            pl.BlockSpec((tm, tn), lambda t: (fin_tile(t), second_col(t, 4))),
        ],
        out_shape=[
            jax.ShapeDtypeStruct((N_TOKENS, ATTN_WIDTH), BF16),
            jax.ShapeDtypeStruct((N_TOKENS, KV_COLS), BF16),
            jax.ShapeDtypeStruct((N_TOKENS, CONV_WIDTH), BF16),
            jax.ShapeDtypeStruct((N_TOKENS, CONV_WIDTH), BF16),
        ],
        scratch_shapes=[pltpu.VMEM((tm, D_MODEL), BF16)] * 2 + [pltpu.VMEM((tm, tn), F32)] * 6,
        compiler_params=_params(1),
        name="in_proj",
    )(x_lat, x_ctx, mods, mods, n1g, w_in, w_in, qg, kg, cos_t, sa_t, sb_t)


_NQ = SEQ // ATTN_TQ


def _attend(q, chunks):
    def scores(chunk):
        k_ref, _, start, size = chunk
        return lax.dot_general(q, k_ref[start:start + size, :], (((1,), (1,)), ((), ())),
                               preferred_element_type=F32)

    m = acc = None
    s = scores(chunks[0])
    for idx, (_, v_ref, start, size) in enumerate(chunks):
        s_next = scores(chunks[idx + 1]) if idx + 1 < len(chunks) else None
        m_c = jnp.max(s, axis=-1, keepdims=True)
        m_new = m_c if m is None else jnp.maximum(m, m_c)
        p = jnp.exp2(s - m_new).astype(BF16)
        pv = jnp.dot(p, v_ref[start:start + size, :], preferred_element_type=F32)
        acc = pv if m is None else jnp.exp2(m - m_new) * acc + pv
        m, s = m_new, s_next
    return acc[:, :HEAD_DIM] / acc[:, HEAD_DIM:]


def _attend_heads(q_ref, o_ref, chunks):
    tq = q_ref.shape[0]
    q = jnp.concatenate([q_ref[:, r * HEAD_DIM:(r + 1) * HEAD_DIM] for r in range(Q_PER_KV)], axis=0)
    o = _attend(q, chunks).astype(BF16)
    for r in range(Q_PER_KV):
        o_ref[:, r * HEAD_DIM:(r + 1) * HEAD_DIM] = o[r * tq:(r + 1) * tq, :]


def _attn_latent_kernel(q_ref, kl_ref, vl_ref, kc_ref, vc_ref, o_ref):
    chunks = [(kl_ref, vl_ref, c * ATTN_TK, ATTN_TK) for c in range(SEQ // ATTN_TK)]
    _attend_heads(q_ref, o_ref, chunks + [(kc_ref, vc_ref, 0, CTX_LEN)])


def _attn_context_kernel(q_ref, kc_ref, vc_ref, o_ref):
    _attend_heads(q_ref, o_ref, [(kc_ref, vc_ref, 0, CTX_LEN)])


def _attention(q, kv):
    tq = ATTN_TQ
    ctx_blk0 = N_LATENT // CTX_LEN
    v_blk0 = KV_WIDTH // (2 * HEAD_DIM)
    width = Q_PER_KV * HEAD_DIM

    attn = pl.pallas_call(
        _attn_latent_kernel,
        grid=(BATCH, N_KV_HEADS, _NQ),
        in_specs=[
            pl.BlockSpec((tq, width), lambda b, g, qi: (b * _NQ + qi, g)),
            pl.BlockSpec((SEQ, HEAD_DIM), lambda b, g, qi: (b, g)),
            pl.BlockSpec((SEQ, 2 * HEAD_DIM), lambda b, g, qi: (b, v_blk0 + g)),
            pl.BlockSpec((CTX_LEN, HEAD_DIM), lambda b, g, qi: (ctx_blk0 + b, g)),
            pl.BlockSpec((CTX_LEN, 2 * HEAD_DIM), lambda b, g, qi: (ctx_blk0 + b, v_blk0 + g)),
        ],
        out_specs=pl.BlockSpec((tq, width), lambda b, g, qi: (b * _NQ + qi, g)),
        out_shape=jax.ShapeDtypeStruct((N_LATENT, ATTN_WIDTH), BF16),
        compiler_params=_params(3),
        name="attention",
    )(q, kv, kv, kv, kv)
    attn_ctx = pl.pallas_call(
        _attn_context_kernel,
        grid=(BATCH, N_KV_HEADS),
        in_specs=[
            pl.BlockSpec((CTX_LEN, width), lambda b, g: (ctx_blk0 + b, g)),
            pl.BlockSpec((CTX_LEN, HEAD_DIM), lambda b, g: (ctx_blk0 + b, g)),
            pl.BlockSpec((CTX_LEN, 2 * HEAD_DIM), lambda b, g: (ctx_blk0 + b, v_blk0 + g)),
        ],
        out_specs=pl.BlockSpec((CTX_LEN, width), lambda b, g: (b, g)),
        out_shape=jax.ShapeDtypeStruct((N_CTX, ATTN_WIDTH), BF16),
        compiler_params=_params(2),
        name="attention_ctx",
    )(q, kv, kv)
    return attn, attn_ctx


def _merge_kernel(xl_ref, xc_ref, attn_lat_ref, attn_ctx_ref, z_ref, zp_ref, zn_ref, gb_ref, cw_ref, cb_ref,
                  ag_ref, cg_ref, w_ref, g1_ref, sh2_ref, sc2_ref, n2g_ref, xo_ref, h2_ref, cat_scr, *, split_stream):
    tm = xo_ref.shape[0]
    i = pl.program_id(0)
    is_ctx = i * tm >= N_LATENT
    attn = jnp.where(is_ctx, attn_ctx_ref[...], attn_lat_ref[...])

    row = lax.broadcasted_iota(jnp.int32, (tm, 1), 0)
    grow = i * tm + row
    seg = jnp.where(grow >= N_LATENT, CTX_LEN, SEQ)
    pos = jnp.bitwise_and(grow, seg - 1)

    z = z_ref[...].astype(F32)
    z_prev = jnp.where(row == 0, zp_ref[HALO - 1:HALO, :].astype(F32), pltpu.roll(z, 1, axis=0))
    z_prev = jnp.where(pos != 0, z_prev, 0.0)
    z_next = jnp.where(row == tm - 1, zn_ref[0:1, :].astype(F32), pltpu.roll(z, tm - 1, axis=0))
    z_next = jnp.where(pos != seg - 1, z_next, 0.0)
    conv = z_prev * cw_ref[0:1, :] + z * cw_ref[1:2, :] + z_next * cw_ref[2:3, :] + cb_ref[...]
    conv = gb_ref[...].astype(F32) * conv

    cat_scr[:, :ATTN_WIDTH] = _rms(attn.astype(F32), ag_ref[...]).astype(BF16)
    cat_scr[:, ATTN_WIDTH:] = _rms(conv, cg_ref[...]).astype(BF16)
    y = jnp.dot(cat_scr[...], w_ref[...], preferred_element_type=F32)
    x = jnp.where(is_ctx, xc_ref[...], xl_ref[...]) if split_stream else xl_ref[...]
    x_new = x + g1_ref[...] * y
    xo_ref[...] = x_new
    h2 = _rms(x_new, n2g_ref[...]) * (1.0 + sc2_ref[...]) + sh2_ref[...]
    h2_ref[...] = h2.astype(BF16)


def _merge(stream, attn_lat, attn_ctx, z, gb, mods, layer, cw, cb, ag, cg, w_out, n2g, n_rows):
    tm = MERGE_TM
    hb = tm // HALO
    last_hb = N_TOKENS // HALO - 1
    lat_tiles = N_LATENT // tm
    assert N_LATENT % tm == 0 and N_CTX % tm == 0
    x_lat, x_ctx = stream
    split = x_ctx is not None
    if split:
        x_specs = [pl.BlockSpec((tm, D_MODEL), lambda i: (jnp.minimum(i, lat_tiles - 1), 0)),
                   pl.BlockSpec((tm, D_MODEL), lambda i: (jnp.maximum(i - lat_tiles, 0), 0))]
    else:
        x_specs = [pl.BlockSpec((tm, D_MODEL), lambda i: (i, 0)), pl.BlockSpec((HALO, D_MODEL), lambda i: (0, 0))]
        x_ctx = x_lat

    def mod_spec(chunk):
        return pl.BlockSpec((None, None, 1, D_MODEL), lambda i: (layer, _mod_row(i, tm), 0, chunk))

    def vec_spec(rows, width):
        return pl.BlockSpec((rows, width), lambda i: (0, 0))

    return pl.pallas_call(
        functools.partial(_merge_kernel, split_stream=split),
        grid=(n_rows // tm,),
        in_specs=x_specs + [
            pl.BlockSpec((tm, ATTN_WIDTH), lambda i: (jnp.minimum(i, lat_tiles - 1), 0)),
            pl.BlockSpec((tm, ATTN_WIDTH), lambda i: (jnp.maximum(i - lat_tiles, 0), 0)),
            pl.BlockSpec((tm, CONV_WIDTH), lambda i: (i, 0)),
            pl.BlockSpec((HALO, CONV_WIDTH), lambda i: (jnp.maximum(i * hb - 1, 0), 0)),
            pl.BlockSpec((HALO, CONV_WIDTH), lambda i: (jnp.minimum((i + 1) * hb, last_hb), 0)),
            pl.BlockSpec((tm, CONV_WIDTH), lambda i: (i, 0)),
            vec_spec(3, CONV_WIDTH), vec_spec(1, CONV_WIDTH),
            vec_spec(1, ATTN_WIDTH), vec_spec(1, CONV_WIDTH),
            pl.BlockSpec((None, D_MODEL, D_MODEL), lambda i: (layer, 0, 0), pipeline_mode=pl.Buffered(1)),
            mod_spec(2), mod_spec(3), mod_spec(4),
            vec_spec(1, D_MODEL),
        ],
        out_specs=[
            pl.BlockSpec((tm, D_MODEL), lambda i: (i, 0)),
            pl.BlockSpec((tm, D_MODEL), lambda i: (i, 0)),
        ],
        out_shape=[
            jax.ShapeDtypeStruct((n_rows, D_MODEL), F32),
            jax.ShapeDtypeStruct((n_rows, D_MODEL), BF16),
        ],
        scratch_shapes=[pltpu.VMEM((tm, D_MODEL), BF16)],
        compiler_params=_params(1),
        name="merge_out_proj",
    )(x_lat, x_ctx, attn_lat, attn_ctx, z, z, z, gb, cw, cb, ag, cg, w_out, mods, mods, mods, n2g)


def _mlp_kernel(h_ref, w1_ref, w2_ref, x_hbm, g2_ref, o_ref, x_buf, x_sem):
    i, k = pl.program_id(0), pl.program_id(1)
    tm = o_ref.shape[0]
    last_k = pl.num_programs(1) - 1

    def x_copy():
        return pltpu.make_async_copy(x_hbm.at[pl.ds(pl.multiple_of(i * tm, tm), tm), :], x_buf, x_sem)

    @pl.when(k == 0)
    def _():
        x_copy().start()
        o_ref[...] = jnp.zeros_like(o_ref)

    def up(s):
        return jnp.dot(h_ref[...], w1_ref[:, s * MLP_SUB:(s + 1) * MLP_SUB], preferred_element_type=F32)

    n_sub = w1_ref.shape[1] // MLP_SUB
    acts = []
    a_next = up(0)
    for s in range(n_sub):
        a = a_next
        if s + 1 < n_sub:
            a_next = up(s + 1)
        a = jnp.maximum(a, 0.0)
        acts.append((a * a).astype(BF16))
    p = jnp.concatenate(acts, axis=1)
    for n in range(D_MODEL // MLP_OUT_SLAB):
        cols = slice(n * MLP_OUT_SLAB, (n + 1) * MLP_OUT_SLAB)
        o_ref[:, cols] += jnp.dot(p, w2_ref[:, cols], preferred_element_type=F32)

    @pl.when(k == last_k)
    def _():
        x_copy().wait()
        o_ref[...] = x_buf[...] + g2_ref[...] * o_ref[...]


def _mlp(h2, x_new, mods, layer, w1, w2, n_rows):
    tm, th = MLP_TM, MLP_TH
    return pl.pallas_call(
        _mlp_kernel,
        grid=(n_rows // tm, MLP_HIDDEN // th),
        in_specs=[
            pl.BlockSpec((tm, D_MODEL), lambda i, k: (i, 0)),
            pl.BlockSpec((None, D_MODEL, th), lambda i, k: (layer, 0, k)),
            pl.BlockSpec((None, th, D_MODEL), lambda i, k: (layer, k, 0)),
            pl.BlockSpec(memory_space=pl.ANY),
            pl.BlockSpec((None, None, 1, D_MODEL), lambda i, k: (layer, _mod_row(i, tm), 0, 5)),
        ],
        out_specs=pl.BlockSpec((tm, D_MODEL), lambda i, k: (i, 0)),
        out_shape=jax.ShapeDtypeStruct((n_rows, D_MODEL), F32),
        scratch_shapes=[pltpu.VMEM((tm, D_MODEL), F32), pltpu.SemaphoreType.DMA(())],
        compiler_params=_params(2),
        name="mlp",
    )(h2, w1, w2, x_new, mods)


def _rope_tables(tm):
    t = jnp.arange(SEQ)
    row = (t // GRID_W).astype(F32)
    col = (t % GRID_W).astype(F32)
    inv_freq = ROPE_THETA ** (-jnp.arange(0, AXIS_DIM, 2, dtype=F32) / AXIS_DIM)
    ang_r = row[:, None] * inv_freq[None, :]
    ang_c = col[:, None] * inv_freq[None, :]
    ang = jnp.concatenate([ang_r, ang_r, ang_c, ang_c], axis=-1)
    cos, sin = jnp.cos(ang), jnp.sin(ang)
    first = (jnp.arange(HEAD_DIM) % AXIS_DIM) < AXIS_DIM // 2
    sin_a = jnp.where(first, -sin, 0.0)
    sin_b = jnp.where(first, 0.0, sin)
    ident = jnp.zeros((tm, HEAD_DIM), F32)
    return (jnp.concatenate([cos, ident + 1.0]), jnp.concatenate([sin_a, ident]),
            jnp.concatenate([sin_b, ident]))


def kernel(x, c, ctx, c_ctx, w_ada, b_ada, norm1_g, w_in, q_norm_g, k_norm_g, conv_w, conv_b,
           attn_out_g, conv_out_g, w_out, norm2_g, w_mlp_in, w_mlp_out):
    assert x.shape == (BATCH, SEQ, D_MODEL) and ctx.shape == (BATCH, CTX_LEN, D_MODEL)
    stream = (x.reshape(N_LATENT, D_MODEL), ctx.reshape(N_CTX, D_MODEL))
    cond = jnp.concatenate([c, c_ctx[None, :], jnp.zeros((MOD_ROWS - BATCH - 1, D_MODEL), F32)], axis=0)
    mods = _adaln_all(cond, w_ada, b_ada).reshape(DEPTH, MOD_ROWS, 1, N_MOD * D_MODEL)
    cos_t, sa_t, sb_t = _rope_tables(INPROJ_TM)
    w_in, w_out, w_mlp_in, w_mlp_out = (w.astype(BF16) for w in (w_in, w_out, w_mlp_in, w_mlp_out))

    for l in range(DEPTH):
        q, kv, z, gb = _in_proj(stream, mods, l, norm1_g[l][None], w_in,
                                q_norm_g[l][None], k_norm_g[l][None], cos_t, sa_t, sb_t)
        attn_lat, attn_ctx = _attention(q, kv)
        n_rows = N_TOKENS if l < DEPTH - 1 else N_LATENT
        x_new, h2 = _merge(stream, attn_lat, attn_ctx, z, gb, mods, l, conv_w[l], conv_b[l][None],
                           attn_out_g[l][None], conv_out_g[l][None], w_out, norm2_g[l][None], n_rows)
        xs = _mlp(h2, x_new, mods, l, w_mlp_in, w_mlp_out, n_rows)
        stream = (xs, None)
    return xs.reshape(BATCH, SEQ, D_MODEL)
```

```python
import functools

import jax
import jax.numpy as jnp
from jax import lax
from jax.experimental import pallas as pl
from jax.experimental.pallas import tpu as pltpu

D_MODEL = 2048
BATCH = 4
SEQ = 4096
DEPTH = 4
CTX_LEN = 256
GRID_W = 64
HEAD_DIM = 128
N_HEADS = 8
N_KV_HEADS = 2
Q_PER_KV = N_HEADS // N_KV_HEADS
ATTN_WIDTH = N_HEADS * HEAD_DIM
KV_WIDTH = N_KV_HEADS * HEAD_DIM
CONV_WIDTH = D_MODEL - ATTN_WIDTH
MLP_HIDDEN = 4 * D_MODEL
N_MOD = 6
ROPE_THETA = 10000.0
AXIS_DIM = HEAD_DIM // 2
EPS = 1e-6
LOG2_E = 1.4426950408889634
KV_COLS = KV_WIDTH + 2 * KV_WIDTH

N_LATENT = BATCH * SEQ
N_CTX = BATCH * CTX_LEN
N_TOKENS = N_LATENT + N_CTX
MOD_ROWS = 8

F32 = jnp.float32
BF16 = jnp.bfloat16

V7X_VMEM_BYTES = 64 * 1024 * 1024
V7X_BF16_SUBLANES = 16

ADA_TN = 1024
INPROJ_TM = 1024
INPROJ_TN = 512
ATTN_TQ = 512
ATTN_TK = 512
MERGE_ROWS = 1024
MERGE_NB = 4
MLP_TM = 1024
MLP_TH = 1024
MLP_SUB = 512
MLP_OUT_SLAB = 512
HALO = V7X_BF16_SUBLANES
VMEM_LIMIT = V7X_VMEM_BYTES * 7 // 8


def _params(n_axes, vmem=VMEM_LIMIT):
    return pltpu.CompilerParams(dimension_semantics=("arbitrary",) * n_axes, vmem_limit_bytes=vmem)


def _mod_row(tile, tm):
    return jnp.minimum(tile // (SEQ // tm), BATCH)


def _rms(x, g):
    return x * lax.rsqrt(jnp.mean(x * x, axis=-1, keepdims=True) + EPS) * g


def _ada_kernel(cond_ref, w_ref, b_ref, o_ref):
    cnd = cond_ref[...]
    s = (cnd * jax.nn.sigmoid(cnd)).astype(BF16)
    w = w_ref[...].astype(BF16)
    o_ref[...] = jnp.dot(s, w, preferred_element_type=F32) + b_ref[...]


def _adaln_all(cond, w_ada, b_ada):
    n = N_MOD * D_MODEL
    return pl.pallas_call(
        _ada_kernel,
        grid=(DEPTH, n // ADA_TN),
        in_specs=[
            pl.BlockSpec((MOD_ROWS, D_MODEL), lambda l, j: (0, 0)),
            pl.BlockSpec((None, D_MODEL, ADA_TN), lambda l, j: (l, 0, j)),
            pl.BlockSpec((None, 1, ADA_TN), lambda l, j: (l, 0, j)),
        ],
        out_specs=pl.BlockSpec((None, MOD_ROWS, ADA_TN), lambda l, j: (l, 0, j)),
        out_shape=jax.ShapeDtypeStruct((DEPTH, MOD_ROWS, n), F32),
        compiler_params=_params(2),
        name="adaln",
    )(cond, w_ada, b_ada.reshape(DEPTH, 1, n))


_KVB = ATTN_WIDTH // INPROJ_TN
_GBB = (ATTN_WIDTH + 2 * KV_WIDTH) // INPROJ_TN
_GCB = _GBB + CONV_WIDTH // INPROJ_TN
_UB = _GCB + CONV_WIDTH // INPROJ_TN
assert _KVB == 2 and _GCB - _GBB == 2 and 2 * KV_WIDTH == INPROJ_TN
_PAIRS = ((0, 1), (_KVB, _GCB), (_UB, _GCB + 1), (_UB + 1, _GBB), (_GBB + 1, None))
_NP = len(_PAIRS)
_SLOT = tuple(j % 3 for j in range(_NP))
_NORM_ROWS = INPROJ_TM // (_NP - 1)


def _lookup(values, idx):
    out = jnp.int32(values[0])
    for n, v in enumerate(values[1:], start=1):
        out = jnp.where(idx == n, jnp.int32(v), out)
    return out


def _rope(x, cos, sin_a, sin_b):
    return (x * cos + pltpu.roll(x, HEAD_DIM - AXIS_DIM // 2, axis=1) * sin_a
            + pltpu.roll(x, AXIS_DIM // 2, axis=1) * sin_b)


def _inproj_kernel(xl_ref, xc_ref, sh_ref, sc_ref, n1g_ref, wa_ref, wb_ref, qg_ref, kg_ref, cos_ref, sa_ref, sb_ref,
                   q_ref, kv_ref, z_ref, gb_ref, h_cur, h_next, ra0, ra1, ra2, rb0, rb1, rb2, *, split_stream):
    t = pl.program_id(0)
    j = t % _NP
    tm = q_ref.shape[0]
    ra, rb = (ra0, ra1, ra2), (rb0, rb1, rb2)

    def normalised():
        x = xl_ref[...]
        if split_stream:
            norm_tile = jnp.minimum(t // _NP, N_TOKENS // tm - 1)
            x = jnp.where(norm_tile * tm >= N_LATENT, xc_ref[...], x)
        h = _rms(x, n1g_ref[...]) * (1.0 + sc_ref[...]) + sh_ref[...]
        return h.astype(BF16)

    def head(r_ref, col0, gain, scale):
        y = _rms(r_ref[:, col0:col0 + HEAD_DIM], gain)
        y = _rope(y, cos_ref[...], sa_ref[...], sb_ref[...])
        return (y * scale).astype(BF16)

    def multiply(step):
        slot = _SLOT[step]
        ra[slot][...] = jnp.dot(h_cur[...], wa_ref[...], preferred_element_type=F32)
        if _PAIRS[step][1] is not None:
            rb[slot][...] = jnp.dot(h_cur[...], wb_ref[...], preferred_element_type=F32)

    def finish(step):
        if step == 1:
            for half, r_ref in enumerate((ra[_SLOT[0]], rb[_SLOT[0]])):
                for hd in range(INPROJ_TN // HEAD_DIM):
                    c0 = half * INPROJ_TN + hd * HEAD_DIM
                    q_ref[:, c0:c0 + HEAD_DIM] = head(r_ref, hd * HEAD_DIM, qg_ref[...],
                                                      HEAD_DIM ** -0.5 * LOG2_E)
        elif step == 2:
            r_ref = ra[_SLOT[1]]
            for hd in range(N_KV_HEADS):
                kv_ref[:, hd * HEAD_DIM:(hd + 1) * HEAD_DIM] = head(r_ref, hd * HEAD_DIM, kg_ref[...], 1.0)
                v0 = KV_WIDTH + hd * 2 * HEAD_DIM
                kv_ref[:, v0:v0 + HEAD_DIM] = r_ref[:, KV_WIDTH + hd * HEAD_DIM:
                                                    KV_WIDTH + (hd + 1) * HEAD_DIM].astype(BF16)
                kv_ref[:, v0 + HEAD_DIM:v0 + 2 * HEAD_DIM] = jnp.ones((tm, HEAD_DIM), BF16)
        elif step == 3:
            z_ref[...] = (rb[_SLOT[1]][...] * ra[_SLOT[2]][...]).astype(BF16)
        elif step == 4:
            z_ref[...] = (rb[_SLOT[2]][...] * ra[_SLOT[3]][...]).astype(BF16)
            gb_ref[...] = rb[_SLOT[3]][...].astype(BF16)
        elif step == 0:
            gb_ref[...] = ra[_SLOT[4]][...].astype(BF16)

    @pl.when(jnp.logical_and(t >= 1, t < _NP))
    def _():
        r0 = pl.multiple_of((t - 1) * _NORM_ROWS, _NORM_ROWS)
        h_next[pl.ds(r0, _NORM_ROWS), :] = normalised()

    @pl.when(t == _NP - 1)
    def _():
        h_cur[...] = h_next[...]

    @pl.when(t == _NP)
    def _():
        multiply(0)

    for step in range(_NP):
        @pl.when(jnp.logical_and(t > _NP, j == step))
        def _(step=step):
            multiply(step)
            if step >= 1:
                h_next[(step - 1) * _NORM_ROWS:step * _NORM_ROWS, :] = normalised()
            finish(step)
            if step == _NP - 1:
                h_cur[...] = h_next[...]


def _in_proj(stream, mods, layer, n1g, w_in, qg, kg, cos_t, sa_t, sb_t):
    tm, tn = INPROJ_TM, INPROJ_TN
    n_tiles = N_TOKENS // tm
    lat_tiles = SEQ // tm
    slices = _NP - 1
    assert tm % slices == 0 and _NORM_ROWS % 16 == 0
    x_lat, x_ctx = stream
    split = x_ctx is not None
    lat_blocks = N_LATENT // _NORM_ROWS

    def tile_of(t):
        return t // _NP - 1

    def norm_block(t):
        tile = jnp.clip(tile_of(t) + 1, 0, n_tiles - 1)
        return tile * slices + jnp.clip(t % _NP - 1, 0, slices - 1)

    def fin_tile(t):
        return jnp.clip(tile_of(t) - (t % _NP == 0).astype(jnp.int32), 0, n_tiles - 1)

    def mod_spec(chunk):
        return pl.BlockSpec((None, None, 1, D_MODEL),
                            lambda t: (layer, _mod_row(norm_block(t) // slices, tm), 0, chunk))

    def w_spec(side):
        blocks = [p[side] if p[side] is not None else p[0] for p in _PAIRS]
        return pl.BlockSpec((None, D_MODEL, tn), lambda t: (layer, 0, _lookup(blocks, t % _NP)))

    def tab_spec():
        def index(t):
            i = fin_tile(t)
            return (jnp.where(i < N_LATENT // tm, i % lat_tiles, lat_tiles), 0)
        return pl.BlockSpec((tm, HEAD_DIM), index)

    def second_col(t, first_step):
        j = t % _NP
        return jnp.logical_and(t > _NP, jnp.logical_or(j == 0, j > first_step)).astype(jnp.int32)

    if split:
        x_specs = [pl.BlockSpec((_NORM_ROWS, D_MODEL), lambda t: (jnp.minimum(norm_block(t), lat_blocks - 1), 0)),
                   pl.BlockSpec((_NORM_ROWS, D_MODEL), lambda t: (jnp.maximum(norm_block(t) - lat_blocks, 0), 0))]
    else:
        x_specs = [pl.BlockSpec((_NORM_ROWS, D_MODEL), lambda t: (norm_block(t), 0)),
                   pl.BlockSpec((HALO, D_MODEL), lambda t: (0, 0))]
        x_ctx = x_lat

    return pl.pallas_call(
        functools.partial(_inproj_kernel, split_stream=split),
        grid=((n_tiles + 1) * _NP + 1,),
        in_specs=x_specs + [
            mod_spec(0), mod_spec(1),
            pl.BlockSpec((1, D_MODEL), lambda t: (0, 0)),
            w_spec(0), w_spec(1),
            pl.BlockSpec((1, HEAD_DIM), lambda t: (0, 0)),
            pl.BlockSpec((1, HEAD_DIM), lambda t: (0, 0)),
            tab_spec(), tab_spec(), tab_spec(),
        ],
        out_specs=[
            pl.BlockSpec((tm, ATTN_WIDTH), lambda t: (fin_tile(t), 0)),
            pl.BlockSpec((tm, KV_COLS), lambda t: (fin_tile(t), 0)),
            pl.BlockSpec((tm, tn), lambda t: (fin_tile(t), second_col(t, 3))),
            pl.BlockSpec((tm, tn), lambda t: (fin_tile(t), second_col(t, 4))),
        ],
        out_shape=[
            jax.ShapeDtypeStruct((N_TOKENS, ATTN_WIDTH), BF16),
            jax.ShapeDtypeStruct((N_TOKENS, KV_COLS), BF16),
            jax.ShapeDtypeStruct((N_TOKENS, CONV_WIDTH), BF16),
            jax.ShapeDtypeStruct((N_TOKENS, CONV_WIDTH), BF16),
        ],
        scratch_shapes=[pltpu.VMEM((tm, D_MODEL), BF16)] * 2 + [pltpu.VMEM((tm, tn), F32)] * 6,
        compiler_params=_params(1),
        name="in_proj",
    )(x_lat, x_ctx, mods, mods, n1g, w_in, w_in, qg, kg, cos_t, sa_t, sb_t)


_NQ = SEQ // ATTN_TQ


def _attend(q, chunks):
    def scores(chunk):
        k_ref, _, start, size = chunk
        return lax.dot_general(q, k_ref[start:start + size, :], (((1,), (1,)), ((), ())),
                               preferred_element_type=F32)

    m = acc = None
    s = scores(chunks[0])
    for idx, (_, v_ref, start, size) in enumerate(chunks):
        s_next = scores(chunks[idx + 1]) if idx + 1 < len(chunks) else None
        m_c = jnp.max(s, axis=-1, keepdims=True)
        m_new = m_c if m is None else jnp.maximum(m, m_c)
        p = jnp.exp2(s - m_new).astype(BF16)
        pv = jnp.dot(p, v_ref[start:start + size, :], preferred_element_type=F32)
        acc = pv if m is None else jnp.exp2(m - m_new) * acc + pv
        m, s = m_new, s_next
    return acc[:, :HEAD_DIM] / acc[:, HEAD_DIM:]


def _attend_heads(q_ref, o_ref, chunks):
    tq = q_ref.shape[0]
    q = jnp.concatenate([q_ref[:, r * HEAD_DIM:(r + 1) * HEAD_DIM] for r in range(Q_PER_KV)], axis=0)
    o = _attend(q, chunks).astype(BF16)
    for r in range(Q_PER_KV):
        o_ref[:, r * HEAD_DIM:(r + 1) * HEAD_DIM] = o[r * tq:(r + 1) * tq, :]


def _attn_latent_kernel(q_ref, kl_ref, vl_ref, kc_ref, vc_ref, o_ref):
    chunks = [(kl_ref, vl_ref, c * ATTN_TK, ATTN_TK) for c in range(SEQ // ATTN_TK)]
    _attend_heads(q_ref, o_ref, chunks + [(kc_ref, vc_ref, 0, CTX_LEN)])


def _attn_context_kernel(q_ref, kc_ref, vc_ref, o_ref):
    _attend_heads(q_ref, o_ref, [(kc_ref, vc_ref, 0, CTX_LEN)])


def _attention(q, kv):
    tq = ATTN_TQ
    ctx_blk0 = N_LATENT // CTX_LEN
    v_blk0 = KV_WIDTH // (2 * HEAD_DIM)
    width = Q_PER_KV * HEAD_DIM

    attn = pl.pallas_call(
        _attn_latent_kernel,
        grid=(BATCH, N_KV_HEADS, _NQ),
        in_specs=[
            pl.BlockSpec((tq, width), lambda b, g, qi: (b * _NQ + qi, g)),
            pl.BlockSpec((SEQ, HEAD_DIM), lambda b, g, qi: (b, g)),
            pl.BlockSpec((SEQ, 2 * HEAD_DIM), lambda b, g, qi: (b, v_blk0 + g)),
            pl.BlockSpec((CTX_LEN, HEAD_DIM), lambda b, g, qi: (ctx_blk0 + b, g)),
            pl.BlockSpec((CTX_LEN, 2 * HEAD_DIM), lambda b, g, qi: (ctx_blk0 + b, v_blk0 + g)),
        ],
        out_specs=pl.BlockSpec((tq, width), lambda b, g, qi: (b * _NQ + qi, g)),
        out_shape=jax.ShapeDtypeStruct((N_LATENT, ATTN_WIDTH), BF16),
        compiler_params=_params(3),
        name="attention",
    )(q, kv, kv, kv, kv)
    attn_ctx = pl.pallas_call(
        _attn_context_kernel,
        grid=(BATCH, N_KV_HEADS),
        in_specs=[
            pl.BlockSpec((CTX_LEN, width), lambda b, g: (ctx_blk0 + b, g)),
            pl.BlockSpec((CTX_LEN, HEAD_DIM), lambda b, g: (ctx_blk0 + b, g)),
            pl.BlockSpec((CTX_LEN, 2 * HEAD_DIM), lambda b, g: (ctx_blk0 + b, v_blk0 + g)),
        ],
        out_specs=pl.BlockSpec((CTX_LEN, width), lambda b, g: (b, g)),
        out_shape=jax.ShapeDtypeStruct((N_CTX, ATTN_WIDTH), BF16),
        compiler_params=_params(2),
        name="attention_ctx",
    )(q, kv, kv)
    return attn, attn_ctx


def _merge_kernel(xl_ref, xc_ref, attn_lat_ref, attn_ctx_ref, z_ref, zp_ref, zn_ref, gb_ref, cw_ref, cb_ref,
                  ag_ref, cg_ref, w_ref, g1_ref, sh2_ref, sc2_ref, n2g_ref, xo_ref, h2_ref, cat0, cat1, y0, y1,
                  stage, *, split_stream, n_tiles):
    t = pl.program_id(0)
    tile = t // MERGE_NB - 1
    j = t % MERGE_NB
    rs = xo_ref.shape[0]
    cat, y = (cat0, cat1), (y0, y1)

    @pl.when(t == 0)
    def _():
        for buf in (cat0, cat1, y0, y1):
            buf[...] = jnp.zeros_like(buf)

    def prepare(dst, jj):
        row0 = jnp.clip(tile + 1, 0, n_tiles - 1) * MERGE_ROWS + jj * rs
        row = lax.broadcasted_iota(jnp.int32, (rs, 1), 0)
        grow = row0 + row
        seg = jnp.where(grow >= N_LATENT, CTX_LEN, SEQ)
        pos = jnp.bitwise_and(grow, seg - 1)

        rows = slice(jj * rs, (jj + 1) * rs)
        is_ctx = row0 >= N_LATENT
        n_chunks = 8
        cw_ = CONV_WIDTH // n_chunks
        ssq_a = ssq_c = None
        for c in range(n_chunks):
            cols = slice(c * cw_, (c + 1) * cw_)
            z = z_ref[:, cols].astype(F32)
            z_prev = jnp.where(row == 0, zp_ref[HALO - 1:HALO, cols].astype(F32), pltpu.roll(z, 1, axis=0))
            z_prev = jnp.where(pos != 0, z_prev, 0.0)
            z_next = jnp.where(row == rs - 1, zn_ref[0:1, cols].astype(F32), pltpu.roll(z, rs - 1, axis=0))
            z_next = jnp.where(pos != seg - 1, z_next, 0.0)
            conv = (z_prev * cw_ref[0:1, cols] + z * cw_ref[1:2, cols] + z_next * cw_ref[2:3, cols]
                    + cb_ref[:, cols])
            conv = gb_ref[:, cols].astype(F32) * conv
            attn = jnp.where(is_ctx, attn_ctx_ref[:, cols], attn_lat_ref[:, cols]).astype(F32)
            stage[:, cols] = attn
            stage[:, ATTN_WIDTH + c * cw_:ATTN_WIDTH + (c + 1) * cw_] = conv
            pa = jnp.sum(attn * attn, axis=-1, keepdims=True)
            pc = jnp.sum(conv * conv, axis=-1, keepdims=True)
            ssq_a = pa if ssq_a is None else ssq_a + pa
            ssq_c = pc if ssq_c is None else ssq_c + pc
        inv_a = lax.rsqrt(ssq_a * (1.0 / ATTN_WIDTH) + EPS)
        inv_c = lax.rsqrt(ssq_c * (1.0 / CONV_WIDTH) + EPS)
        for c in range(n_chunks):
            cols = slice(c * cw_, (c + 1) * cw_)
            ccols = slice(ATTN_WIDTH + c * cw_, ATTN_WIDTH + (c + 1) * cw_)
            dst[rows, cols] = (stage[:, cols] * inv_a * ag_ref[:, cols]).astype(BF16)
            dst[rows, ccols] = (stage[:, ccols] * inv_c * cg_ref[:, cols]).astype(BF16)

    def finish(src, jj):
        x = xl_ref[...]
        if split_stream:
            row0 = jnp.clip(tile - 1, 0, n_tiles - 1) * MERGE_ROWS + jj * rs
            x = jnp.where(row0 >= N_LATENT, xc_ref[...], x)
        n_chunks = 16
        cw_ = D_MODEL // n_chunks
        ssq = None
        for c in range(n_chunks):
            cols = slice(c * cw_, (c + 1) * cw_)
            x_new = x[:, cols] + g1_ref[:, cols] * src[jj * rs:(jj + 1) * rs, cols]
            xo_ref[:, cols] = x_new
            part = jnp.sum(x_new * x_new, axis=-1, keepdims=True)
            ssq = part if ssq is None else ssq + part
        inv = lax.rsqrt(ssq * (1.0 / D_MODEL) + EPS)
        for c in range(n_chunks):
            cols = slice(c * cw_, (c + 1) * cw_)
            h2 = (xo_ref[:, cols] * inv * n2g_ref[:, cols]) * (1.0 + sc2_ref[:, cols]) + sh2_ref[:, cols]
            h2_ref[:, cols] = h2.astype(BF16)

    tn = w_ref.shape[1]
    for p in range(2):
        for jj in range(MERGE_NB):
            @pl.when(jnp.logical_and((tile + 2) % 2 == p, j == jj))
            def _(p=p, jj=jj):
                y[p][:, jj * tn:(jj + 1) * tn] = jnp.dot(cat[p][...], w_ref[...], preferred_element_type=F32)
                prepare(cat[1 - p], jj)
                finish(y[1 - p], jj)


def _merge(stream, attn_lat, attn_ctx, z, gb, mods, layer, cw, cb, ag, cg, w_out, n2g, n_rows):
    rs = MERGE_ROWS // MERGE_NB
    tn = D_MODEL // MERGE_NB
    n_tiles = n_rows // MERGE_ROWS
    n_blocks = n_rows // rs
    hb = rs // HALO
    last_hb = N_TOKENS // HALO - 1
    lat_blocks = N_LATENT // rs
    assert n_rows % MERGE_ROWS == 0 and N_LATENT % rs == 0 and N_CTX % rs == 0

    def tile_of(t):
        return t // MERGE_NB - 1

    def pblk(t):
        return jnp.clip(tile_of(t) + 1, 0, n_tiles - 1) * MERGE_NB + t % MERGE_NB

    def fblk(t):
        return jnp.clip((tile_of(t) - 1) * MERGE_NB + t % MERGE_NB, 0, n_blocks - 1)

    x_lat, x_ctx = stream
    split = x_ctx is not None
    if split:
        x_specs = [pl.BlockSpec((rs, D_MODEL), lambda t: (jnp.minimum(fblk(t), lat_blocks - 1), 0)),
                   pl.BlockSpec((rs, D_MODEL), lambda t: (jnp.maximum(fblk(t) - lat_blocks, 0), 0))]
    else:
        x_specs = [pl.BlockSpec((rs, D_MODEL), lambda t: (fblk(t), 0)), pl.BlockSpec((HALO, D_MODEL), lambda t: (0, 0))]
        x_ctx = x_lat

    def mod_spec(chunk):
        return pl.BlockSpec((None, None, 1, D_MODEL),
                            lambda t: (layer, _mod_row(jnp.clip(tile_of(t) - 1, 0, n_tiles - 1), MERGE_ROWS), 0, chunk))

    def vec_spec(rows, width):
        return pl.BlockSpec((rows, width), lambda t: (0, 0))

    return pl.pallas_call(
        functools.partial(_merge_kernel, split_stream=split, n_tiles=n_tiles),
        grid=((n_tiles + 2) * MERGE_NB,),
        in_specs=x_specs + [
            pl.BlockSpec((rs, ATTN_WIDTH), lambda t: (jnp.minimum(pblk(t), lat_blocks - 1), 0)),
            pl.BlockSpec((rs, ATTN_WIDTH), lambda t: (jnp.maximum(pblk(t) - lat_blocks, 0), 0)),
            pl.BlockSpec((rs, CONV_WIDTH), lambda t: (pblk(t), 0)),
            pl.BlockSpec((HALO, CONV_WIDTH), lambda t: (jnp.maximum(pblk(t) * hb - 1, 0), 0)),
            pl.BlockSpec((HALO, CONV_WIDTH), lambda t: (jnp.minimum((pblk(t) + 1) * hb, last_hb), 0)),
            pl.BlockSpec((rs, CONV_WIDTH), lambda t: (pblk(t), 0)),
            vec_spec(3, CONV_WIDTH), vec_spec(1, CONV_WIDTH),
            vec_spec(1, ATTN_WIDTH), vec_spec(1, CONV_WIDTH),
            pl.BlockSpec((None, D_MODEL, tn), lambda t: (layer, 0, t % MERGE_NB)),
            mod_spec(2), mod_spec(3), mod_spec(4),
            vec_spec(1, D_MODEL),
        ],
        out_specs=[
            pl.BlockSpec((rs, D_MODEL), lambda t: (fblk(t), 0)),
            pl.BlockSpec((rs, D_MODEL), lambda t: (fblk(t), 0)),
        ],
        out_shape=[
            jax.ShapeDtypeStruct((n_rows, D_MODEL), F32),
            jax.ShapeDtypeStruct((n_rows, D_MODEL), BF16),
        ],
        scratch_shapes=[pltpu.VMEM((MERGE_ROWS, D_MODEL), BF16)] * 2 + [pltpu.VMEM((MERGE_ROWS, D_MODEL), F32)] * 2
        + [pltpu.VMEM((rs, D_MODEL), F32)],
        compiler_params=_params(1),
        name="merge_out_proj",
    )(x_lat, x_ctx, attn_lat, attn_ctx, z, z, z, gb, cw, cb, ag, cg, w_out, mods, mods, mods, n2g)


def _mlp_kernel(h_ref, w1_ref, w2_ref, x_hbm, g2_ref, o_ref, x_buf, x_sem):
    i, k = pl.program_id(0), pl.program_id(1)
    tm = o_ref.shape[0]
    last_k = pl.num_programs(1) - 1

    def x_copy():
        return pltpu.make_async_copy(x_hbm.at[pl.ds(pl.multiple_of(i * tm, tm), tm), :], x_buf, x_sem)

    def up(s):
        return jnp.dot(h_ref[...], w1_ref[:, s * MLP_SUB:(s + 1) * MLP_SUB], preferred_element_type=F32)

    def hidden_step(first, last):
        n_sub = w1_ref.shape[1] // MLP_SUB
        acts = []
        a_next = up(0)
        for s in range(n_sub):
            a = a_next
            if s + 1 < n_sub:
                a_next = up(s + 1)
            a = jnp.maximum(a, 0.0)
            acts.append((a * a).astype(BF16))
        p = jnp.concatenate(acts, axis=1)
        for n in range(D_MODEL // MLP_OUT_SLAB):
            cols = slice(n * MLP_OUT_SLAB, (n + 1) * MLP_OUT_SLAB)
            part = jnp.dot(p, w2_ref[:, cols], preferred_element_type=F32)
            total = part if first else o_ref[:, cols] + part
            o_ref[:, cols] = x_buf[:, cols] + g2_ref[:, cols] * total if last else total

    assert MLP_HIDDEN // w1_ref.shape[1] >= 2

    @pl.when(k == 0)
    def _():
        x_copy().start()
        hidden_step(first=True, last=False)

    @pl.when(jnp.logical_and(k > 0, k < last_k))
    def _():
        hidden_step(first=False, last=False)

    @pl.when(k == last_k)
    def _():
        x_copy().wait()
        hidden_step(first=False, last=True)


def _mlp(h2, x_new, mods, layer, w1, w2, n_rows):
    tm, th = MLP_TM, MLP_TH
    return pl.pallas_call(
        _mlp_kernel,
        grid=(n_rows // tm, MLP_HIDDEN // th),
        in_specs=[
            pl.BlockSpec((tm, D_MODEL), lambda i, k: (i, 0)),
            pl.BlockSpec((None, D_MODEL, th), lambda i, k: (layer, 0, k)),
            pl.BlockSpec((None, th, D_MODEL), lambda i, k: (layer, k, 0)),
            pl.BlockSpec(memory_space=pl.ANY),
            pl.BlockSpec((None, None, 1, D_MODEL), lambda i, k: (layer, _mod_row(i, tm), 0, 5)),
        ],
        out_specs=pl.BlockSpec((tm, D_MODEL), lambda i, k: (i, 0)),
        out_shape=jax.ShapeDtypeStruct((n_rows, D_MODEL), F32),
        scratch_shapes=[pltpu.VMEM((tm, D_MODEL), F32), pltpu.SemaphoreType.DMA(())],
        compiler_params=_params(2),
        name="mlp",
    )(h2, w1, w2, x_new, mods)


def _rope_tables(tm):
    t = jnp.arange(SEQ)
    row = (t // GRID_W).astype(F32)
    col = (t % GRID_W).astype(F32)
    inv_freq = ROPE_THETA ** (-jnp.arange(0, AXIS_DIM, 2, dtype=F32) / AXIS_DIM)
    ang_r = row[:, None] * inv_freq[None, :]
    ang_c = col[:, None] * inv_freq[None, :]
    ang = jnp.concatenate([ang_r, ang_r, ang_c, ang_c], axis=-1)
    cos, sin = jnp.cos(ang), jnp.sin(ang)
    first = (jnp.arange(HEAD_DIM) % AXIS_DIM) < AXIS_DIM // 2
    sin_a = jnp.where(first, -sin, 0.0)
    sin_b = jnp.where(first, 0.0, sin)
    ident = jnp.zeros((tm, HEAD_DIM), F32)
    return (jnp.concatenate([cos, ident + 1.0]), jnp.concatenate([sin_a, ident]),
            jnp.concatenate([sin_b, ident]))


def kernel(x, c, ctx, c_ctx, w_ada, b_ada, norm1_g, w_in, q_norm_g, k_norm_g, conv_w, conv_b,
           attn_out_g, conv_out_g, w_out, norm2_g, w_mlp_in, w_mlp_out):
    assert x.shape == (BATCH, SEQ, D_MODEL) and ctx.shape == (BATCH, CTX_LEN, D_MODEL)
    stream = (x.reshape(N_LATENT, D_MODEL), ctx.reshape(N_CTX, D_MODEL))
    cond = jnp.concatenate([c, c_ctx[None, :], jnp.zeros((MOD_ROWS - BATCH - 1, D_MODEL), F32)], axis=0)
    mods = _adaln_all(cond, w_ada, b_ada).reshape(DEPTH, MOD_ROWS, 1, N_MOD * D_MODEL)
    cos_t, sa_t, sb_t = _rope_tables(INPROJ_TM)
    w_in, w_out, w_mlp_in, w_mlp_out = (w.astype(BF16) for w in (w_in, w_out, w_mlp_in, w_mlp_out))

    for l in range(DEPTH):
        q, kv, z, gb = _in_proj(stream, mods, l, norm1_g[l][None], w_in,
                                q_norm_g[l][None], k_norm_g[l][None], cos_t, sa_t, sb_t)
        attn_lat, attn_ctx = _attention(q, kv)
        n_rows = N_TOKENS if l < DEPTH - 1 else N_LATENT
        x_new, h2 = _merge(stream, attn_lat, attn_ctx, z, gb, mods, l, conv_w[l], conv_b[l][None],
                           attn_out_g[l][None], conv_out_g[l][None], w_out, norm2_g[l][None], n_rows)
        xs = _mlp(h2, x_new, mods, l, w_mlp_in, w_mlp_out, n_rows)
        stream = (xs, None)
    return xs.reshape(BATCH, SEQ, D_MODEL)
```
